```python
import math
import jax, jax.numpy as jnp
from jax import lax
import numpy as np

D_MODEL = 1024
BATCH = 16
SEQ = 2048
DEPTH = 1

MEM_LEN = 256
D_S5 = D_MODEL
S5_CH_PER_GROUP = 16
S5_GROUPS = D_S5 // S5_CH_PER_GROUP
S5_STATE = 64
D_M2 = 2 * D_MODEL
M2_HEADDIM = 64
M2_HEADS = D_M2 // M2_HEADDIM
M2_GROUPS = 4
M2_STATE = 128
M2_CONV = 4
M2_CHUNK = 128
D_XBC = D_M2 + 2 * M2_GROUPS * M2_STATE
XA_HEADS = 4
XA_HEADDIM = 128
D_XA = XA_HEADS * XA_HEADDIM
N_BRANCH = 3
D_FF = 2816
FFN_CONV = 3
D_IN = D_S5 + D_M2 + D_XBC + M2_HEADS + D_XA + N_BRANCH * D_MODEL
EPS = 1e-6

kernel_name = "hybrid_s5_ssd_xattn_gated_merge_convffn"


def rmsnorm(x, g):
    xf = x.astype(jnp.float32)
    inv = lax.rsqrt(jnp.mean(xf * xf, axis=-1, keepdims=True) + EPS)
    return (xf * inv).astype(x.dtype) * g


def causal_dwconv(u, w, b):
    K, C = w.shape
    y = lax.conv_general_dilated(
        u, w[:, None, :].astype(u.dtype), window_strides=(1,), padding=[(K - 1, 0)],
        dimension_numbers=('NWC', 'WIO', 'NWC'), feature_group_count=C)
    return y + b


def s5_mixer(u, lam_re, lam_im, log_dt, b_re, b_im, c_re, c_im, d):
    bsz, L, _ = u.shape
    ug = u.reshape(bsz, L, S5_GROUPS, S5_CH_PER_GROUP)
    dt = jnp.exp(log_dt)[:, None]
    mag = jnp.exp(lam_re * dt)
    ar = mag * jnp.cos(lam_im * dt)
    ai = mag * jnp.sin(lam_im * dt)
    den = lam_re * lam_re + lam_im * lam_im
    fr = ((ar - 1.0) * lam_re + ai * lam_im) / den
    fi = (ai * lam_re - (ar - 1.0) * lam_im) / den
    bbr = fr[..., None] * b_re - fi[..., None] * b_im
    bbi = fr[..., None] * b_im + fi[..., None] * b_re
    bu_re = jnp.einsum('blgk,gpk->blgp', ug, bbr)
    bu_im = jnp.einsum('blgk,gpk->blgp', ug, bbi)
    a_re = jnp.broadcast_to(ar, (1, L) + ar.shape)
    a_im = jnp.broadcast_to(ai, (1, L) + ai.shape)

    def combine(e1, e2):
        a1r, a1i, b1r, b1i = e1
        a2r, a2i, b2r, b2i = e2
        return (a2r * a1r - a2i * a1i,
                a2r * a1i + a2i * a1r,
                a2r * b1r - a2i * b1i + b2r,
                a2r * b1i + a2i * b1r + b2i)

    _, _, xr, xi = lax.associative_scan(combine, (a_re, a_im, bu_re, bu_im), axis=1)
    y = jnp.einsum('gkp,blgp->blgk', c_re, xr) - jnp.einsum('gkp,blgp->blgk', c_im, xi)
    return y.reshape(bsz, L, D_S5) + d * u


def ssd_chunked(xh, dt, A, Bm, Cm):
    bsz, L, H, P = xh.shape
    G, N = Bm.shape[-2:]
    R = H // G
    nc = L // M2_CHUNK
    x = xh.reshape(bsz, nc, M2_CHUNK, G, R, P)
    dtc = dt.reshape(bsz, nc, M2_CHUNK, G, R)
    dA = (dtc * A.reshape(G, R)).astype(jnp.float32)
    Bc = Bm.reshape(bsz, nc, M2_CHUNK, G, N)
    Cc = Cm.reshape(bsz, nc, M2_CHUNK, G, N)
    a_cum = jnp.cumsum(dA, axis=2)
    xdt = x * dtc[..., None]
    seg = a_cum[:, :, :, None] - a_cum[:, :, None, :]
    causal = jnp.tril(jnp.ones((M2_CHUNK, M2_CHUNK), dtype=bool))
    lmat = jnp.exp(jnp.where(causal[:, :, None, None], seg, -jnp.inf))
    cb = jnp.einsum('bclgn,bcsgn->bclsg', Cc, Bc)
    y_diag = jnp.einsum('bclsg,bclsgr,bcsgrp->bclgrp', cb, lmat, xdt)
    decay_s = jnp.exp(a_cum[:, :, -1:] - a_cum)
    states = jnp.einsum('bclgn,bclgr,bclgrp->bcgrpn', Bc, decay_s, xdt)
    chunk_decay = jnp.exp(a_cum[:, :, -1])

    def step(h, inp):
        s, dec = inp
        return h * dec[..., None, None] + s, h

    h0 = jnp.zeros((bsz, G, R, P, N), dtype=states.dtype)
    _, states_in = lax.scan(step, h0, (jnp.moveaxis(states, 1, 0), jnp.moveaxis(chunk_decay, 1, 0)))
    states_in = jnp.moveaxis(states_in, 0, 1)
    y_off = jnp.einsum('bclgn,bcgrpn,bclgr->bclgrp', Cc, states_in, jnp.exp(a_cum))
    return (y_diag + y_off).reshape(bsz, L, H * P)


def mamba2_mixer(z, xbc, dt_raw, conv_w, conv_b, dt_bias, a_log, d, norm_w):
    bsz, L, _ = z.shape
    xbc = jax.nn.silu(causal_dwconv(xbc, conv_w, conv_b))
    xs = xbc[..., :D_M2]
    Bm = xbc[..., D_M2:D_M2 + M2_GROUPS * M2_STATE].reshape(bsz, L, M2_GROUPS, M2_STATE)
    Cm = xbc[..., D_M2 + M2_GROUPS * M2_STATE:].reshape(bsz, L, M2_GROUPS, M2_STATE)
    dt = jax.nn.softplus((dt_raw + dt_bias).astype(jnp.float32))
    A = -jnp.exp(a_log.astype(jnp.float32))
    xh = xs.reshape(bsz, L, M2_HEADS, M2_HEADDIM)
    y = ssd_chunked(xh, dt, A, Bm, Cm)
    y = y + (xh * d[:, None]).reshape(bsz, L, D_M2)
    return rmsnorm(y * jax.nn.silu(z), norm_w)


def memory_attention(q_flat, mem, norm_mem, w_kv):
    bsz, L, _ = q_flat.shape
    mem_n = rmsnorm(mem, norm_mem)
    kv = mem_n @ w_kv
    k = kv[..., :D_XA].reshape(bsz, MEM_LEN, XA_HEADS, XA_HEADDIM)
    v = kv[..., D_XA:].reshape(bsz, MEM_LEN, XA_HEADS, XA_HEADDIM)
    q = q_flat.reshape(bsz, L, XA_HEADS, XA_HEADDIM)
    s = jnp.einsum('bqhd,bkhd->bhqk', q, k) * (XA_HEADDIM ** -0.5)
    p = jax.nn.softmax(s.astype(jnp.float32), axis=-1).astype(v.dtype)
    return jnp.einsum('bhqk,bkhd->bqhd', p, v).reshape(bsz, L, D_XA)


def setup_inputs(seed: int = 0) -> dict:
    key = jax.random.key(seed)
    ks = jax.random.split(key, 32)
    f32 = jnp.float32

    def nrm(k, shape, scale):
        return jax.random.normal(k, shape, f32) * scale

    def gain(k, n):
        return 1.0 + 0.02 * jax.random.normal(k, (DEPTH, n), f32)

    log_lo, log_hi = math.log(1e-3), math.log(1e-1)
    m2_dt = jnp.exp(jax.random.uniform(ks[14], (DEPTH, M2_HEADS), f32, log_lo, log_hi))
    return {
        "x": nrm(ks[0], (BATCH, SEQ, D_MODEL), 1.0),
        "mem": nrm(ks[1], (BATCH, MEM_LEN, D_MODEL), 1.0),
        "norm_mix": gain(ks[2], D_MODEL),
        "w_in": nrm(ks[3], (DEPTH, D_MODEL, D_IN), D_MODEL ** -0.5),
        "s5_lambda_re": -0.5 + 0.01 * jax.random.normal(ks[4], (DEPTH, S5_GROUPS, S5_STATE), f32),
        "s5_lambda_im": jnp.broadcast_to(jnp.pi * jnp.arange(S5_STATE, dtype=f32), (DEPTH, S5_GROUPS, S5_STATE)),
        "s5_log_dt": jax.random.uniform(ks[5], (DEPTH, S5_GROUPS), f32, log_lo, log_hi),
        "s5_b_re": nrm(ks[6], (DEPTH, S5_GROUPS, S5_STATE, S5_CH_PER_GROUP), (2 * S5_CH_PER_GROUP) ** -0.5),
        "s5_b_im": nrm(ks[7], (DEPTH, S5_GROUPS, S5_STATE, S5_CH_PER_GROUP), (2 * S5_CH_PER_GROUP) ** -0.5),
        "s5_c_re": nrm(ks[8], (DEPTH, S5_GROUPS, S5_CH_PER_GROUP, S5_STATE), S5_STATE ** -0.5),
        "s5_c_im": nrm(ks[9], (DEPTH, S5_GROUPS, S5_CH_PER_GROUP, S5_STATE), S5_STATE ** -0.5),
        "s5_d": nrm(ks[10], (DEPTH, D_S5), 1.0),
        "w_a_val": nrm(ks[11], (DEPTH, D_S5, D_MODEL), D_S5 ** -0.5),
        "w_a_gate": nrm(ks[12], (DEPTH, D_S5, D_MODEL), D_S5 ** -0.5),
        "m2_conv_w": nrm(ks[13], (DEPTH, M2_CONV, D_XBC), M2_CONV ** -0.5),
        "m2_conv_b": nrm(ks[15], (DEPTH, D_XBC), 0.02),
        "m2_dt_bias": m2_dt + jnp.log(-jnp.expm1(-m2_dt)),
        "m2_a_log": jnp.log(jax.random.uniform(ks[16], (DEPTH, M2_HEADS), f32, 1.0, 16.0)),
        "m2_d": gain(ks[17], M2_HEADS),
        "m2_norm": gain(ks[18], D_M2),
        "w_b": nrm(ks[19], (DEPTH, D_M2, D_MODEL), D_M2 ** -0.5),
        "norm_mem": gain(ks[20], D_MODEL),
        "w_kv": nrm(ks[21], (DEPTH, D_MODEL, 2 * D_XA), D_MODEL ** -0.5),
        "w_c": nrm(ks[22], (DEPTH, D_XA, D_MODEL), D_XA ** -0.5),
        "w_out": nrm(ks[23], (DEPTH, D_MODEL, D_MODEL), D_MODEL ** -0.5),
        "norm_ffn": gain(ks[24], D_MODEL),
        "w_up": nrm(ks[25], (DEPTH, D_MODEL, 2 * D_FF), D_MODEL ** -0.5),
        "ffn_conv_w": nrm(ks[26], (DEPTH, FFN_CONV, 2 * D_FF), FFN_CONV ** -0.5),
        "ffn_conv_b": nrm(ks[27], (DEPTH, 2 * D_FF), 0.02),
        "w_down": nrm(ks[28], (DEPTH, D_FF, D_MODEL), D_FF ** -0.5),
        "norm_final": 1.0 + 0.02 * jax.random.normal(ks[29], (D_MODEL,), f32),
    }


def reference(x, mem, norm_mix, w_in, s5_lambda_re, s5_lambda_im, s5_log_dt, s5_b_re, s5_b_im,
              s5_c_re, s5_c_im, s5_d, w_a_val, w_a_gate, m2_conv_w, m2_conv_b, m2_dt_bias,
              m2_a_log, m2_d, m2_norm, w_b, norm_mem, w_kv, w_c, w_out, norm_ffn, w_up,
              ffn_conv_w, ffn_conv_b, w_down, norm_final):
    p1 = D_S5
    p2 = p1 + D_M2
    p3 = p2 + D_XBC
    p4 = p3 + M2_HEADS
    p5 = p4 + D_XA
    for l in range(DEPTH):
        h = rmsnorm(x, norm_mix[l])
        proj = h @ w_in[l]
        u_s5 = proj[..., :p1]
        z = proj[..., p1:p2]
        xbc = proj[..., p2:p3]
        dt_raw = proj[..., p3:p4]
        q = proj[..., p4:p5]
        gates = jax.nn.sigmoid(proj[..., p5:])

        y_s5 = s5_mixer(u_s5, s5_lambda_re[l], s5_lambda_im[l], s5_log_dt[l], s5_b_re[l],
                        s5_b_im[l], s5_c_re[l], s5_c_im[l], s5_d[l])
        g_s5 = jax.nn.gelu(y_s5)
        y_a = (g_s5 @ w_a_val[l]) * jax.nn.sigmoid(g_s5 @ w_a_gate[l])
        y_b = mamba2_mixer(z, xbc, dt_raw, m2_conv_w[l], m2_conv_b[l], m2_dt_bias[l],
                           m2_a_log[l], m2_d[l], m2_norm[l]) @ w_b[l]
        y_c = memory_attention(q, mem, norm_mem[l], w_kv[l]) @ w_c[l]

        g_a = gates[..., :D_MODEL]
        g_b = gates[..., D_MODEL:2 * D_MODEL]
        g_c = gates[..., 2 * D_MODEL:]
        x = x + (g_a * y_a + g_b * y_b + g_c * y_c) @ w_out[l]

        h = rmsnorm(x, norm_ffn[l])
        up = causal_dwconv(h @ w_up[l], ffn_conv_w[l], ffn_conv_b[l])
        x = x + (jax.nn.silu(up[..., :D_FF]) * up[..., D_FF:]) @ w_down[l]
    return rmsnorm(x, norm_final)
```

```python
import functools
import math

import jax
import jax.numpy as jnp
from jax import lax
from jax.experimental import pallas as pl
from jax.experimental.pallas import tpu as pltpu

D_MODEL = 1024
MEM_LEN = 256
S5_CH_PER_GROUP = 16
S5_GROUPS = 64
S5_STATE = 64
S5_GROUPS_PER_BLOCK = 16
S5_BLOCKS = S5_GROUPS // S5_GROUPS_PER_BLOCK
S5_BLOCK_CH = S5_GROUPS_PER_BLOCK * S5_CH_PER_GROUP
S5_BLOCK_STATE = S5_GROUPS_PER_BLOCK * S5_STATE
D_M2 = 2048
M2_HEADDIM = 64
M2_HEADS = 32
M2_GROUPS = 4
M2_HEADS_PER_GROUP = M2_HEADS // M2_GROUPS
M2_STATE = 128
M2_CONV = 4
M2_CHUNK = 128
D_BC = M2_GROUPS * M2_STATE
D_XBC = D_M2 + 2 * D_BC
XA_HEADS = 4
XA_HEADDIM = 128
D_XA = 512
D_FF = 2816
FFN_CONV = 3
D_GATES = 3 * D_MODEL
DT_PAD = 128
EPS = 1e-6

_O_XBC = 0
_O_GATES = _O_XBC + D_XBC
_O_Z = _O_GATES + D_GATES
_O_Q = _O_Z + D_M2
_O_U = _O_Q + D_XA
_O_DT = _O_U + D_MODEL
_W_IN_COLS = _O_DT + DT_PAD

_VMEM_LIMIT = 56 * 1024 * 1024
_F32 = jnp.float32
_BF16 = jnp.bfloat16


def _rms_f32(xf, g):
    inv = lax.rsqrt(jnp.mean(xf * xf, axis=-1, keepdims=True) + EPS)
    return xf * inv * g


def _sigmoid(x):
    return 1.0 / (1.0 + jnp.exp(-x))


def _dot(a, b):
    return jnp.dot(a, b, preferred_element_type=_F32)


def _resident(shape):
    nd = len(shape)
    return pl.BlockSpec(shape, lambda *_: (0,) * nd, pipeline_mode=pl.Buffered(1))


def _inproj_kernel(x_ref, g_ref, w_ref, xbc_ref, gates_ref, z_ref, q_ref, dt_ref, u_ref):
    h = _rms_f32(x_ref[...], g_ref[...]).astype(_BF16)
    step = 512

    def proj(lo, width, out_ref, act=None, dtype=_BF16):
        for c in range(0, width, step):
            w = min(step, width - c)
            r = _dot(h, w_ref[:, lo + c:lo + c + w])
            if act is not None:
                r = act(r)
            out_ref[:, c:c + w] = r.astype(dtype)

    proj(_O_XBC, D_XBC, xbc_ref)
    proj(_O_GATES, D_GATES, gates_ref, act=_sigmoid)
    proj(_O_Z, D_M2, z_ref)
    proj(_O_Q, D_XA, q_ref)
    proj(_O_U, D_MODEL, u_ref)
    proj(_O_DT, DT_PAD, dt_ref, dtype=_F32)


def _inproj(x2, norm_g, w_in_r, bsz, seq, tm):
    nt = seq // tm
    tok = bsz * seq
    row = lambda b, t: (b * nt + t, 0)
    return pl.pallas_call(
        _inproj_kernel,
        grid=(bsz, nt),
        in_specs=[
            pl.BlockSpec((tm, D_MODEL), row),
            _resident((1, D_MODEL)),
            _resident((D_MODEL, _W_IN_COLS)),
        ],
        out_specs=[
            pl.BlockSpec((tm, D_XBC), row),
            pl.BlockSpec((tm, D_GATES), row),
            pl.BlockSpec((tm, D_M2), row),
            pl.BlockSpec((tm, D_XA), row),
            pl.BlockSpec((tm, DT_PAD), row),
            pl.BlockSpec((tm, D_MODEL), lambda b, t: (t, b)),
        ],
        out_shape=[
            jax.ShapeDtypeStruct((tok, D_XBC), _BF16),
            jax.ShapeDtypeStruct((tok, D_GATES), _BF16),
            jax.ShapeDtypeStruct((tok, D_M2), _BF16),
            jax.ShapeDtypeStruct((tok, D_XA), _BF16),
            jax.ShapeDtypeStruct((tok, DT_PAD), _F32),
            jax.ShapeDtypeStruct((seq, bsz * D_MODEL), _BF16),
        ],
        compiler_params=pltpu.CompilerParams(
            dimension_semantics=("arbitrary", "arbitrary"), vmem_limit_bytes=_VMEM_LIMIT),
        name="inproj",
    )(x2, norm_g, w_in_r)


def _gelu_tanh(x):
    c = math.sqrt(2.0 / math.pi)
    return 0.5 * x * (1.0 + jnp.tanh(c * (x + 0.044715 * (x * x * x))))


def _s5_kernel(u_ref, bdb_ref, bdc_ref, are_ref, aim_ref, d_ref, out_ref, bu, state, *, bsz, tt, lane_chunk):
    ns = S5_BLOCK_STATE

    @pl.when(pl.program_id(1) == 0)
    def _():
        state[...] = jnp.zeros_like(state)

    u = u_ref[...]
    bu[...] = _dot(u, bdb_ref[...])

    for c in range(0, ns, lane_chunk):
        ar = jnp.broadcast_to(are_ref[:, c:c + lane_chunk], (bsz, lane_chunk))
        ai = jnp.broadcast_to(aim_ref[:, c:c + lane_chunk], (bsz, lane_chunk))

        def body(t, carry, c=c, ar=ar, ai=ai):
            xr, xi = carry
            r0 = pl.multiple_of(t * bsz, bsz)
            br = bu[pl.ds(r0, bsz), c:c + lane_chunk]
            bi = bu[pl.ds(r0, bsz), ns + c:ns + c + lane_chunk]
            nxr = ar * xr - ai * xi + br
            nxi = ar * xi + ai * xr + bi
            bu[pl.ds(r0, bsz), c:c + lane_chunk] = nxr
            bu[pl.ds(r0, bsz), ns + c:ns + c + lane_chunk] = nxi
            return nxr, nxi

        xr, xi = lax.fori_loop(0, tt, body, (state[0, :, c:c + lane_chunk], state[1, :, c:c + lane_chunk]))
        state[0, :, c:c + lane_chunk] = xr
        state[1, :, c:c + lane_chunk] = xi

    y = _dot(bu[...].astype(_BF16), bdc_ref[...]) + d_ref[...] * u.astype(_F32)
    out_ref[...] = _gelu_tanh(y).astype(_BF16)


def _s5(u_t, bdb, bdc, a_re, a_im, d_blk, bsz, seq, tt):
    rows = tt * bsz
    lane_chunk = 512
    kern = functools.partial(_s5_kernel, bsz=bsz, tt=tt, lane_chunk=lane_chunk)
    return pl.pallas_call(
        kern,
        grid=(S5_BLOCKS, seq // tt),
        in_specs=[
            pl.BlockSpec((rows, S5_BLOCK_CH), lambda g, t: (t, g)),
            pl.BlockSpec((None, S5_BLOCK_CH, 2 * S5_BLOCK_STATE), lambda g, t: (g, 0, 0)),
            pl.BlockSpec((None, 2 * S5_BLOCK_STATE, S5_BLOCK_CH), lambda g, t: (g, 0, 0)),
            pl.BlockSpec((None, 1, S5_BLOCK_STATE), lambda g, t: (g, 0, 0)),
            pl.BlockSpec((None, 1, S5_BLOCK_STATE), lambda g, t: (g, 0, 0)),
            pl.BlockSpec((None, 1, S5_BLOCK_CH), lambda g, t: (g, 0, 0)),
        ],
        out_specs=pl.BlockSpec((rows, S5_BLOCK_CH), lambda g, t: (t, g)),
        out_shape=jax.ShapeDtypeStruct((seq * bsz, D_MODEL), _BF16),
        scratch_shapes=[
            pltpu.VMEM((rows, 2 * S5_BLOCK_STATE), _F32),
            pltpu.VMEM((2, bsz, S5_BLOCK_STATE), _F32),
        ],
        compiler_params=pltpu.CompilerParams(
            dimension_semantics=("arbitrary", "arbitrary"), vmem_limit_bytes=_VMEM_LIMIT),
        name="s5",
    )(u_t, bdb, bdc, a_re, a_im, d_blk)


def _softplus(x):
    return jnp.maximum(x, 0.0) + jnp.log(1.0 + jnp.exp(-jnp.abs(x)))


def _mamba_kernel(xbc_ref, z_ref, dt_ref, cw_ref, cb_ref, dtb_ref, a_ref, dexp_ref, nw_ref, e_ref,
                  out_ref, ext, hst, ybuf):
    q = M2_CHUNK
    halo = 8

    @pl.when(pl.program_id(1) == 0)
    def _():
        ext[0:halo, :] = jnp.zeros((halo, D_XBC), _F32)
        hst[...] = jnp.zeros_like(hst)

    ext[halo:halo + q, :] = xbc_ref[...].astype(_F32)
    acc = cb_ref[...] + cw_ref[0:1, :] * ext[pl.ds(halo - 3, q), :]
    for k in range(1, M2_CONV):
        acc = acc + cw_ref[k:k + 1, :] * ext[pl.ds(halo - 3 + k, q), :]
    ext[0:halo, :] = ext[q:q + halo, :]
    xc = acc * _sigmoid(acc)
    xs = xc[:, :D_M2]
    bm = xc[:, D_M2:D_M2 + D_BC].astype(_BF16)
    cm = xc[:, D_M2 + D_BC:].astype(_BF16)

    dt = _softplus(dt_ref[...] + dtb_ref[...])
    da = dt * a_ref[...]
    rowi = lax.broadcasted_iota(jnp.int32, (q, q), 0)
    coli = lax.broadcasted_iota(jnp.int32, (q, q), 1)
    causal = rowi >= coli
    tril = jnp.where(causal, 1.0, 0.0).astype(_F32)
    a_cum = jnp.dot(tril, da, preferred_element_type=_F32, precision=lax.Precision.HIGHEST)
    a_last = a_cum[q - 1:q, :]
    a_cum_t = a_cum.T

    e = e_ref[...]

    def expand(v):
        hi = v.astype(_BF16)
        lo = (v - hi.astype(_F32)).astype(_BF16)
        return _dot(hi, e) + _dot(lo, e)

    dt_e = expand(dt)
    expa_e = expand(jnp.exp(a_cum))
    dec_e = expand(jnp.exp(a_last - a_cum))
    cd_e = expand(jnp.broadcast_to(jnp.exp(a_last), (8, DT_PAD)))[0:1, :]

    xdt = xs * dt_e
    xdt_b = xdt.astype(_BF16)
    xdec_b = (xdt * dec_e).astype(_BF16)
    lane = lax.broadcasted_iota(jnp.int32, (q, 2 * M2_HEADDIM), 1)
    gw = M2_HEADS_PER_GROUP * M2_HEADDIM

    for g in range(M2_GROUPS):
        cg = cm[:, g * M2_STATE:(g + 1) * M2_STATE]
        bg = bm[:, g * M2_STATE:(g + 1) * M2_STATE]
        cb = lax.dot_general(cg, bg, (((1,), (1,)), ((), ())), preferred_element_type=_F32)
        hprev = hst[g]
        yoff = _dot(cg, hprev.astype(_BF16)) * expa_e[:, g * gw:(g + 1) * gw]
        st = lax.dot_general(bg, xdec_b[:, g * gw:(g + 1) * gw], (((0,), (0,)), ((), ())),
                             preferred_element_type=_F32)
        hst[g] = hprev * cd_e[:, g * gw:(g + 1) * gw] + st
        for pair in range(M2_HEADS_PER_GROUP // 2):
            h0 = g * M2_HEADS_PER_GROUP + 2 * pair
            ms = []
            for hh in (h0, h0 + 1):
                seg = a_cum[:, hh:hh + 1] - a_cum_t[hh:hh + 1, :]
                lm = jnp.exp(jnp.where(causal, seg, -1e30))
                ms.append((cb * lm).astype(_BF16))
            lhs = jnp.concatenate(ms, axis=1)
            xp = xdt_b[:, h0 * M2_HEADDIM:(h0 + 2) * M2_HEADDIM]
            zero = jnp.zeros_like(xp)
            rhs = jnp.concatenate([jnp.where(lane < M2_HEADDIM, xp, zero),
                                   jnp.where(lane >= M2_HEADDIM, xp, zero)], axis=0)
            yd = _dot(lhs, rhs)
            c0 = h0 * M2_HEADDIM
            ybuf[:, c0:c0 + 2 * M2_HEADDIM] = yd + yoff[:, pair * 2 * M2_HEADDIM:(pair + 1) * 2 * M2_HEADDIM]

    y = ybuf[...] + xs * dexp_ref[...]
    zf = z_ref[...].astype(_F32)
    v = y * (zf * _sigmoid(zf))
    out_ref[...] = _rms_f32(v, nw_ref[...]).astype(_BF16)


def _mamba(xbc, z, dt, cw, cb, dtb, a_neg, dexp, nw, e, bsz, seq):
    q = M2_CHUNK
    nc = seq // q
    tok = bsz * seq
    row = lambda b, c: (b * nc + c, 0)
    return pl.pallas_call(
        _mamba_kernel,
        grid=(bsz, nc),
        in_specs=[
            pl.BlockSpec((q, D_XBC), row),
            pl.BlockSpec((q, D_M2), row),
            pl.BlockSpec((q, DT_PAD), row),
            _resident((M2_CONV, D_XBC)),
            _resident((1, D_XBC)),
            _resident((1, DT_PAD)),
            _resident((1, DT_PAD)),
            _resident((1, D_M2)),
            _resident((1, D_M2)),
            _resident((DT_PAD, D_M2)),
        ],
        out_specs=pl.BlockSpec((q, D_M2), row),
        out_shape=jax.ShapeDtypeStruct((tok, D_M2), _BF16),
        scratch_shapes=[
            pltpu.VMEM((q + 8, D_XBC), _F32),
            pltpu.VMEM((M2_GROUPS, M2_STATE, M2_HEADS_PER_GROUP * M2_HEADDIM), _F32),
            pltpu.VMEM((q, D_M2), _F32),
        ],
        compiler_params=pltpu.CompilerParams(
            dimension_semantics=("arbitrary", "arbitrary"), vmem_limit_bytes=_VMEM_LIMIT),
        name="mamba",
    )(xbc, z, dt, cw, cb, dtb, a_neg, dexp, nw, e)


def _kv_kernel(mem_ref, g_ref, w_ref, kv_ref):
    mn = _rms_f32(mem_ref[...], g_ref[...]).astype(_BF16)
    kv_ref[...] = _dot(mn, w_ref[...]).astype(_BF16)


def _kv(mem2, g, w_kv, bsz):
    return pl.pallas_call(
        _kv_kernel,
        grid=(bsz,),
        in_specs=[
            pl.BlockSpec((MEM_LEN, D_MODEL), lambda b: (b, 0)),
            _resident((1, D_MODEL)),
            _resident((D_MODEL, 2 * D_XA)),
        ],
        out_specs=pl.BlockSpec((MEM_LEN, 2 * D_XA), lambda b: (b, 0)),
        out_shape=jax.ShapeDtypeStruct((bsz * MEM_LEN, 2 * D_XA), _BF16),
        compiler_params=pltpu.CompilerParams(
            dimension_semantics=("arbitrary",), vmem_limit_bytes=_VMEM_LIMIT),
        name="kv",
    )(mem2, g, w_kv)


def _attn_kernel(q_ref, kv_ref, out_ref):
    scale = XA_HEADDIM ** -0.5
    for h in range(XA_HEADS):
        lo = h * XA_HEADDIM
        qh = q_ref[:, lo:lo + XA_HEADDIM]
        kh = kv_ref[:, lo:lo + XA_HEADDIM]
        vh = kv_ref[:, D_XA + lo:D_XA + lo + XA_HEADDIM]
        s = lax.dot_general(qh, kh, (((1,), (1,)), ((), ())), preferred_element_type=_F32) * scale
        m = jnp.max(s, axis=-1, keepdims=True)
        p = jnp.exp(s - m)
        p = p / jnp.sum(p, axis=-1, keepdims=True)
        out_ref[:, lo:lo + XA_HEADDIM] = _dot(p.astype(_BF16), vh).astype(_BF16)


def _attn(q, kv, bsz, seq, tm):
    nt = seq // tm
    row = lambda b, t: (b * nt + t, 0)
    return pl.pallas_call(
        _attn_kernel,
        grid=(bsz, nt),
        in_specs=[
            pl.BlockSpec((tm, D_XA), row),
            pl.BlockSpec((MEM_LEN, 2 * D_XA), lambda b, t: (b, 0)),
        ],
        out_specs=pl.BlockSpec((tm, D_XA), row),
        out_shape=jax.ShapeDtypeStruct((bsz * seq, D_XA), _BF16),
        compiler_params=pltpu.CompilerParams(
            dimension_semantics=("arbitrary", "arbitrary"), vmem_limit_bytes=_VMEM_LIMIT),
        name="attn",
    )(q, kv)


def _merge_kernel(x_ref, gs5_ref, gates_ref, nb_ref, at_ref, wav_ref, wag_ref, wb_ref, wc_ref, wout_ref,
                  nf_ref, x1_ref, h2_ref):
    g = gs5_ref[...]
    ya = _dot(g, wav_ref[...]) * _sigmoid(_dot(g, wag_ref[...]))
    m = gates_ref[:, 0:D_MODEL].astype(_F32) * ya
    m = m + gates_ref[:, D_MODEL:2 * D_MODEL].astype(_F32) * _dot(nb_ref[...], wb_ref[...])
    m = m + gates_ref[:, 2 * D_MODEL:].astype(_F32) * _dot(at_ref[...], wc_ref[...])
    x1 = x_ref[...] + _dot(m.astype(_BF16), wout_ref[...])
    x1_ref[...] = x1
    h2_ref[...] = _rms_f32(x1, nf_ref[...]).astype(_BF16)


def _merge(x2, gs5_t, gates, nb, at, wav, wag, wb, wc, wout, nf, bsz, seq, tm):
    nt = seq // tm
    tok = bsz * seq
    row = lambda b, t: (b * nt + t, 0)
    return pl.pallas_call(
        _merge_kernel,
        grid=(bsz, nt),
        in_specs=[
            pl.BlockSpec((tm, D_MODEL), row),
            pl.BlockSpec((tm, D_MODEL), lambda b, t: (t, b)),
            pl.BlockSpec((tm, D_GATES), row),
            pl.BlockSpec((tm, D_M2), row),
            pl.BlockSpec((tm, D_XA), row),
            _resident((D_MODEL, D_MODEL)),
            _resident((D_MODEL, D_MODEL)),
            _resident((D_M2, D_MODEL)),
            _resident((D_XA, D_MODEL)),
            _resident((D_MODEL, D_MODEL)),
            _resident((1, D_MODEL)),
        ],
        out_specs=[pl.BlockSpec((tm, D_MODEL), row), pl.BlockSpec((tm, D_MODEL), row)],
        out_shape=[jax.ShapeDtypeStruct((tok, D_MODEL), _F32), jax.ShapeDtypeStruct((tok, D_MODEL), _BF16)],
        compiler_params=pltpu.CompilerParams(
            dimension_semantics=("arbitrary", "arbitrary"), vmem_limit_bytes=_VMEM_LIMIT),
        name="merge",
    )(x2, gs5_t, gates, nb, at, wav, wag, wb, wc, wout, nf)


def _ffn_kernel(h2_ref, halo_ref, x1_ref, wup_ref, cw_ref, cb_ref, wdown_ref, nfin_ref, out_ref, ua, ub,
                *, tm, chunk):
    hl = 16
    first = pl.program_id(1) == 0
    halo = jnp.where(first, jnp.zeros((hl, D_MODEL), _BF16), halo_ref[...])
    h2e = jnp.concatenate([halo, h2_ref[...]], axis=0)
    acc = jnp.zeros((tm, D_MODEL), _F32)
    for j in range(D_FF // chunk):
        ca, cg = j * chunk, D_FF + j * chunk
        ua[...] = _dot(h2e, wup_ref[:, ca:ca + chunk])
        ub[...] = _dot(h2e, wup_ref[:, cg:cg + chunk])
        va = cb_ref[:, ca:ca + chunk]
        vb = cb_ref[:, cg:cg + chunk]
        for k in range(FFN_CONV):
            off = hl - (FFN_CONV - 1) + k
            va = va + cw_ref[k:k + 1, ca:ca + chunk] * ua[pl.ds(off, tm), :]
            vb = vb + cw_ref[k:k + 1, cg:cg + chunk] * ub[pl.ds(off, tm), :]
        act = (va * _sigmoid(va) * vb).astype(_BF16)
        acc = acc + _dot(act, wdown_ref[ca:ca + chunk, :])
    out_ref[...] = _rms_f32(x1_ref[...] + acc, nfin_ref[...])


def _ffn(h2, x1, wup, cw, cb, wdown, nfin, bsz, seq, tm):
    nt = seq // tm
    tok = bsz * seq
    hl = 16
    chunk = 256
    row = lambda b, t: (b * nt + t, 0)
    halo_row = lambda b, t: (jnp.maximum((b * nt + t) * (tm // hl) - 1, 0), 0)
    kern = functools.partial(_ffn_kernel, tm=tm, chunk=chunk)
    return pl.pallas_call(
        kern,
        grid=(bsz, nt),
        in_specs=[
            pl.BlockSpec((tm, D_MODEL), row),
            pl.BlockSpec((hl, D_MODEL), halo_row),
            pl.BlockSpec((tm, D_MODEL), row),
            _resident((D_MODEL, 2 * D_FF)),
            _resident((FFN_CONV, 2 * D_FF)),
            _resident((1, 2 * D_FF)),
            _resident((D_FF, D_MODEL)),
            _resident((1, D_MODEL)),
        ],
        out_specs=pl.BlockSpec((tm, D_MODEL), row),
        out_shape=jax.ShapeDtypeStruct((tok, D_MODEL), _F32),
        scratch_shapes=[pltpu.VMEM((hl + tm, chunk), _F32), pltpu.VMEM((hl + tm, chunk), _F32)],
        compiler_params=pltpu.CompilerParams(
            dimension_semantics=("arbitrary", "arbitrary"), vmem_limit_bytes=_VMEM_LIMIT),
        name="ffn",
    )(h2, h2, x1, wup, cw, cb, wdown, nfin)


def _s5_params(lam_re, lam_im, log_dt, b_re, b_im, c_re, c_im, d):
    dt = jnp.exp(log_dt)[:, None]
    mag = jnp.exp(lam_re * dt)
    ar = mag * jnp.cos(lam_im * dt)
    ai = mag * jnp.sin(lam_im * dt)
    den = lam_re * lam_re + lam_im * lam_im
    fr = ((ar - 1.0) * lam_re + ai * lam_im) / den
    fi = (ai * lam_re - (ar - 1.0) * lam_im) / den
    bbr = fr[..., None] * b_re - fi[..., None] * b_im
    bbi = fr[..., None] * b_im + fi[..., None] * b_re
    eye = jnp.eye(S5_GROUPS_PER_BLOCK, dtype=_F32)
    nb, gl, k, p = S5_BLOCKS, S5_GROUPS_PER_BLOCK, S5_CH_PER_GROUP, S5_STATE

    def bd_in(bb):
        t = bb.transpose(0, 2, 1).reshape(nb, gl, k, p)
        return jnp.einsum('jgkp,gh->jgkhp', t, eye).reshape(nb, gl * k, gl * p)

    def bd_out(c):
        t = c.transpose(0, 2, 1).reshape(nb, gl, p, k)
        return jnp.einsum('jgpk,gh->jgphk', t, eye).reshape(nb, gl * p, gl * k)

    bdb = jnp.concatenate([bd_in(bbr), bd_in(bbi)], axis=-1).astype(_BF16)
    bdc = jnp.concatenate([bd_out(c_re), -bd_out(c_im)], axis=1).astype(_BF16)
    a_re = ar.reshape(nb, 1, gl * p)
    a_im = ai.reshape(nb, 1, gl * p)
    d_blk = d.reshape(nb, 1, gl * k)
    return bdb, bdc, a_re, a_im, d_blk


def kernel(x, mem, norm_mix, w_in, s5_lambda_re, s5_lambda_im, s5_log_dt, s5_b_re, s5_b_im, s5_c_re, s5_c_im,
           s5_d, w_a_val, w_a_gate, m2_conv_w, m2_conv_b, m2_dt_bias, m2_a_log, m2_d, m2_norm, w_b, norm_mem,
           w_kv, w_c, w_out, norm_ffn, w_up, ffn_conv_w, ffn_conv_b, w_down, norm_final):
    bsz, seq, _ = x.shape
    depth = w_in.shape[0]
    tm = min(512, seq)
    tt = min(32, seq)
    p1 = D_MODEL
    p2 = p1 + D_M2
    p3 = p2 + D_XBC
    p4 = p3 + M2_HEADS
    p5 = p4 + D_XA
    row1 = lambda v: v.reshape(1, -1).astype(_F32)
    head_expand = (jnp.arange(DT_PAD)[:, None] == (jnp.arange(D_M2)[None, :] // M2_HEADDIM)).astype(_BF16)
    pad_heads = lambda v: jnp.pad(v.astype(_F32), (0, DT_PAD - M2_HEADS)).reshape(1, DT_PAD)

    assert depth == 1, "single-layer problem: the final RMSNorm is fused into the FFN kernel"
    l = 0
    x2 = x.reshape(bsz * seq, D_MODEL)
    mem2 = mem.reshape(bsz * MEM_LEN, D_MODEL)
    w = w_in[l]
    w_in_r = jnp.concatenate(
        [w[:, p2:p3], w[:, p5:], w[:, p1:p2], w[:, p4:p5], w[:, :p1],
         jnp.pad(w[:, p3:p4], ((0, 0), (0, DT_PAD - M2_HEADS)))], axis=1).astype(_BF16)
    xbc, gates, z, q, dt_raw, u_t = _inproj(x2, row1(norm_mix[l]), w_in_r, bsz, seq, tm)

    bdb, bdc, a_re, a_im, d_blk = _s5_params(
        s5_lambda_re[l], s5_lambda_im[l], s5_log_dt[l], s5_b_re[l], s5_b_im[l], s5_c_re[l], s5_c_im[l],
        s5_d[l])
    gs5_t = _s5(u_t.reshape(seq * bsz, D_MODEL), bdb, bdc, a_re, a_im, d_blk, bsz, seq, tt)

    nb = _mamba(xbc, z, dt_raw, m2_conv_w[l].astype(_F32), row1(m2_conv_b[l]), pad_heads(m2_dt_bias[l]),
                pad_heads(-jnp.exp(m2_a_log[l].astype(_F32))),
                row1(jnp.repeat(m2_d[l], M2_HEADDIM)), row1(m2_norm[l]), head_expand, bsz, seq)

    kv = _kv(mem2, row1(norm_mem[l]), w_kv[l].astype(_BF16), bsz)
    at = _attn(q, kv, bsz, seq, tm)

    x1, h2 = _merge(x2, gs5_t.reshape(seq, bsz * D_MODEL), gates, nb, at,
                    w_a_val[l].astype(_BF16), w_a_gate[l].astype(_BF16), w_b[l].astype(_BF16),
                    w_c[l].astype(_BF16), w_out[l].astype(_BF16), row1(norm_ffn[l]), bsz, seq, tm)

    out = _ffn(h2, x1, w_up[l].astype(_BF16), ffn_conv_w[l].astype(_F32), row1(ffn_conv_b[l]),
               w_down[l].astype(_BF16), row1(norm_final), bsz, seq, tm)
    return out.reshape(bsz, seq, D_MODEL)
```

```python
import functools
import math

import jax
import jax.numpy as jnp
from jax import lax
from jax.experimental import pallas as pl
from jax.experimental.pallas import tpu as pltpu

D_MODEL = 1024
MEM_LEN = 256
S5_CH_PER_GROUP = 16
S5_GROUPS = 64
S5_STATE = 64
S5_GROUPS_PER_BLOCK = 16
S5_BLOCKS = S5_GROUPS // S5_GROUPS_PER_BLOCK
S5_BLOCK_CH = S5_GROUPS_PER_BLOCK * S5_CH_PER_GROUP
S5_BLOCK_STATE = S5_GROUPS_PER_BLOCK * S5_STATE
D_M2 = 2048
M2_HEADDIM = 64
M2_HEADS = 32
M2_GROUPS = 4
M2_HEADS_PER_GROUP = M2_HEADS // M2_GROUPS
M2_STATE = 128
M2_CONV = 4
M2_CHUNK = 128
D_BC = M2_GROUPS * M2_STATE
D_XBC = D_M2 + 2 * D_BC
XA_HEADS = 4
XA_HEADDIM = 128
D_XA = 512
D_FF = 2816
FFN_CONV = 3
D_GATES = 3 * D_MODEL
DT_PAD = 128
EPS = 1e-6

_O_XBC = 0
_O_GATES = _O_XBC + D_XBC
_O_Z = _O_GATES + D_GATES
_O_Q = _O_Z + D_M2
_O_U = _O_Q + D_XA
_O_DT = _O_U + D_MODEL
_W_IN_COLS = _O_DT + DT_PAD

_VMEM_LIMIT = 56 * 1024 * 1024
_F32 = jnp.float32
_BF16 = jnp.bfloat16


def _rms_f32(xf, g):
    inv = lax.rsqrt(jnp.mean(xf * xf, axis=-1, keepdims=True) + EPS)
    return xf * inv * g


def _sigmoid(x):
    return 1.0 / (1.0 + jnp.exp(-x))


def _dot(a, b):
    return jnp.dot(a, b, preferred_element_type=_F32)


def _resident(shape):
    nd = len(shape)
    return pl.BlockSpec(shape, lambda *_: (0,) * nd, pipeline_mode=pl.Buffered(1))


def _inproj_kernel(x_ref, g_ref, w_ref, xbc_ref, gates_ref, z_ref, q_ref, dt_ref, u_ref):
    h = _rms_f32(x_ref[...], g_ref[...]).astype(_BF16)
    step = 512

    def proj(lo, width, out_ref, act=None, dtype=_BF16):
        for c in range(0, width, step):
            w = min(step, width - c)
            r = _dot(h, w_ref[:, lo + c:lo + c + w])
            if act is not None:
                r = act(r)
            out_ref[:, c:c + w] = r.astype(dtype)

    proj(_O_XBC, D_XBC, xbc_ref)
    proj(_O_GATES, D_GATES, gates_ref, act=_sigmoid)
    proj(_O_Z, D_M2, z_ref)
    proj(_O_Q, D_XA, q_ref)
    proj(_O_U, D_MODEL, u_ref)
    proj(_O_DT, DT_PAD, dt_ref, dtype=_F32)


def _inproj(x2, norm_g, w_in_r, bsz, seq, tm):
    nt = seq // tm
    tok = bsz * seq
    row = lambda b, t: (b * nt + t, 0)
    return pl.pallas_call(
        _inproj_kernel,
        grid=(bsz, nt),
        in_specs=[
            pl.BlockSpec((tm, D_MODEL), row),
            _resident((1, D_MODEL)),
            _resident((D_MODEL, _W_IN_COLS)),
        ],
        out_specs=[
            pl.BlockSpec((tm, D_XBC), row),
            pl.BlockSpec((tm, D_GATES), row),
            pl.BlockSpec((tm, D_M2), row),
            pl.BlockSpec((tm, D_XA), row),
            pl.BlockSpec((tm, DT_PAD), row),
            pl.BlockSpec((tm, D_MODEL), row),
        ],
        out_shape=[
            jax.ShapeDtypeStruct((tok, D_XBC), _BF16),
            jax.ShapeDtypeStruct((tok, D_GATES), _BF16),
            jax.ShapeDtypeStruct((tok, D_M2), _BF16),
            jax.ShapeDtypeStruct((tok, D_XA), _BF16),
            jax.ShapeDtypeStruct((tok, DT_PAD), _F32),
            jax.ShapeDtypeStruct((tok, D_MODEL), _BF16),
        ],
        compiler_params=pltpu.CompilerParams(
            dimension_semantics=("arbitrary", "arbitrary"), vmem_limit_bytes=_VMEM_LIMIT),
        name="inproj",
    )(x2, norm_g, w_in_r)


def _gelu_tanh(x):
    c = math.sqrt(2.0 / math.pi)
    return 0.5 * x * (1.0 + jnp.tanh(c * (x + 0.044715 * (x * x * x))))


def _s5_kernel(u_ref, unext_ref, perm_ref, permt_ref, bdb_ref, bdc_ref, are_ref, aim_ref, d_ref, out_ref,
               buf0, buf1, ub0, ub1, state, *, bsz, tt, lane_chunk):
    ns = S5_BLOCK_STATE
    rows = bsz * tt

    def project_in(u_bt, buf, ub):
        u_tb = _dot(perm_ref[...], u_bt.reshape(rows, S5_BLOCK_CH))
        ub[...] = u_tb
        u_b = u_tb.astype(_BF16)
        buf[:, 0:ns] = _dot(u_b, bdb_ref[:, 0:ns])
        buf[:, ns:2 * ns] = _dot(u_b, bdb_ref[:, ns:2 * ns])

    def scan(buf):
        for c in range(0, ns, lane_chunk):
            ar = jnp.broadcast_to(are_ref[:, c:c + lane_chunk], (bsz, lane_chunk))
            ai = jnp.broadcast_to(aim_ref[:, c:c + lane_chunk], (bsz, lane_chunk))
            xr = state[0, :, c:c + lane_chunk]
            xi = state[1, :, c:c + lane_chunk]
            for t in range(tt):
                r = slice(t * bsz, (t + 1) * bsz)
                nxr = ar * xr - ai * xi + buf[r, c:c + lane_chunk]
                nxi = ar * xi + ai * xr + buf[r, ns + c:ns + c + lane_chunk]
                buf[r, c:c + lane_chunk] = nxr
                buf[r, ns + c:ns + c + lane_chunk] = nxi
                xr, xi = nxr, nxi
            state[0, :, c:c + lane_chunk] = xr
            state[1, :, c:c + lane_chunk] = xi

    def project_out(buf, ub, t0):
        y = _dot(buf[:, 0:ns].astype(_BF16), bdc_ref[0:ns, :])
        y = y + _dot(buf[:, ns:2 * ns].astype(_BF16), bdc_ref[ns:2 * ns, :])
        y = y + d_ref[...] * ub[...]
        g_tb = _gelu_tanh(y).astype(_BF16)
        g_bt = _dot(permt_ref[...], g_tb).astype(_BF16)
        out_ref[:, t0:t0 + tt, :] = g_bt.reshape(bsz, tt, S5_BLOCK_CH)

    @pl.when(pl.program_id(1) == 0)
    def _():
        state[...] = jnp.zeros_like(state)
        project_in(u_ref[:, 0:tt, :], buf0, ub0)

    project_in(u_ref[:, tt:2 * tt, :], buf1, ub1)
    scan(buf0)
    project_out(buf0, ub0, 0)
    project_in(unext_ref[...], buf0, ub0)
    scan(buf1)
    project_out(buf1, ub1, tt)


def _s5(u, perm, permt, bdb, bdc, a_re, a_im, d_blk, bsz, seq, tt):
    rows = tt * bsz
    nsteps = seq // (2 * tt)
    last = seq // tt - 1
    kern = functools.partial(_s5_kernel, bsz=bsz, tt=tt, lane_chunk=512)
    blk = lambda g, j: (g, 0, 0)
    return pl.pallas_call(
        kern,
        grid=(S5_BLOCKS, nsteps),
        in_specs=[
            pl.BlockSpec((bsz, 2 * tt, S5_BLOCK_CH), lambda g, j: (0, j, g)),
            pl.BlockSpec((bsz, tt, S5_BLOCK_CH), lambda g, j: (0, jnp.minimum(2 * j + 2, last), g)),
            _resident((rows, rows)),
            _resident((rows, rows)),
            pl.BlockSpec((None, S5_BLOCK_CH, 2 * S5_BLOCK_STATE), blk),
            pl.BlockSpec((None, 2 * S5_BLOCK_STATE, S5_BLOCK_CH), blk),
            pl.BlockSpec((None, 1, S5_BLOCK_STATE), blk),
            pl.BlockSpec((None, 1, S5_BLOCK_STATE), blk),
            pl.BlockSpec((None, 1, S5_BLOCK_CH), blk),
        ],
        out_specs=pl.BlockSpec((bsz, 2 * tt, S5_BLOCK_CH), lambda g, j: (0, j, g)),
        out_shape=jax.ShapeDtypeStruct((bsz, seq, D_MODEL), _BF16),
        scratch_shapes=[
            pltpu.VMEM((rows, 2 * S5_BLOCK_STATE), _F32),
            pltpu.VMEM((rows, 2 * S5_BLOCK_STATE), _F32),
            pltpu.VMEM((rows, S5_BLOCK_CH), _F32),
            pltpu.VMEM((rows, S5_BLOCK_CH), _F32),
            pltpu.VMEM((2, bsz, S5_BLOCK_STATE), _F32),
        ],
        compiler_params=pltpu.CompilerParams(
            dimension_semantics=("arbitrary", "arbitrary"), vmem_limit_bytes=_VMEM_LIMIT),
        name="s5",
    )(u, u, perm, permt, bdb, bdc, a_re, a_im, d_blk)


def _softplus(x):
    return jnp.maximum(x, 0.0) + jnp.log(1.0 + jnp.exp(-jnp.abs(x)))


def _mamba_kernel(xbc_ref, z_ref, dt_ref, cw_ref, cb_ref, dtb_ref, a_ref, dexp_ref, nw_ref, e_ref,
                  out_ref, ext, hst, ybuf):
    q = M2_CHUNK
    halo = 8

    @pl.when(pl.program_id(1) == 0)
    def _():
        ext[0:halo, :] = jnp.zeros((halo, D_XBC), _F32)
        hst[...] = jnp.zeros_like(hst)

    ext[halo:halo + q, :] = xbc_ref[...].astype(_F32)
    acc = cb_ref[...] + cw_ref[0:1, :] * ext[pl.ds(halo - 3, q), :]
    for k in range(1, M2_CONV):
        acc = acc + cw_ref[k:k + 1, :] * ext[pl.ds(halo - 3 + k, q), :]
    ext[0:halo, :] = ext[q:q + halo, :]
    xc = acc * _sigmoid(acc)
    xs = xc[:, :D_M2]
    bm = xc[:, D_M2:D_M2 + D_BC].astype(_BF16)
    cm = xc[:, D_M2 + D_BC:].astype(_BF16)

    dt = _softplus(dt_ref[...] + dtb_ref[...])
    da = dt * a_ref[...]
    rowi = lax.broadcasted_iota(jnp.int32, (q, q), 0)
    coli = lax.broadcasted_iota(jnp.int32, (q, q), 1)
    causal = rowi >= coli
    tril = jnp.where(causal, 1.0, 0.0).astype(_F32)
    a_cum = jnp.dot(tril, da, preferred_element_type=_F32, precision=lax.Precision.HIGHEST)
    a_last = a_cum[q - 1:q, :]
    a_cum_t = a_cum.T

    e = e_ref[...]

    def expand(v):
        hi = v.astype(_BF16)
        lo = (v - hi.astype(_F32)).astype(_BF16)
        return _dot(hi, e) + _dot(lo, e)

    dt_e = expand(dt)
    expa_e = expand(jnp.exp(a_cum))
    dec_e = expand(jnp.exp(a_last - a_cum))
    cd_e = expand(jnp.broadcast_to(jnp.exp(a_last), (8, DT_PAD)))[0:1, :]

    xdt = xs * dt_e
    xdt_b = xdt.astype(_BF16)
    xdec_b = (xdt * dec_e).astype(_BF16)
    lane = lax.broadcasted_iota(jnp.int32, (q, 2 * M2_HEADDIM), 1)
    gw = M2_HEADS_PER_GROUP * M2_HEADDIM

    for g in range(M2_GROUPS):
        cg = cm[:, g * M2_STATE:(g + 1) * M2_STATE]
        bg = bm[:, g * M2_STATE:(g + 1) * M2_STATE]
        cb = lax.dot_general(cg, bg, (((1,), (1,)), ((), ())), preferred_element_type=_F32)
        hprev = hst[g]
        yoff = _dot(cg, hprev.astype(_BF16)) * expa_e[:, g * gw:(g + 1) * gw]
        st = lax.dot_general(bg, xdec_b[:, g * gw:(g + 1) * gw], (((0,), (0,)), ((), ())),
                             preferred_element_type=_F32)
        hst[g] = hprev * cd_e[:, g * gw:(g + 1) * gw] + st
        for pair in range(M2_HEADS_PER_GROUP // 2):
            h0 = g * M2_HEADS_PER_GROUP + 2 * pair
            ms = []
            for hh in (h0, h0 + 1):
                seg = a_cum[:, hh:hh + 1] - a_cum_t[hh:hh + 1, :]
                lm = jnp.exp(jnp.where(causal, seg, -1e30))
                ms.append((cb * lm).astype(_BF16))
            lhs = jnp.concatenate(ms, axis=1)
            xp = xdt_b[:, h0 * M2_HEADDIM:(h0 + 2) * M2_HEADDIM]
            zero = jnp.zeros_like(xp)
            rhs = jnp.concatenate([jnp.where(lane < M2_HEADDIM, xp, zero),
                                   jnp.where(lane >= M2_HEADDIM, xp, zero)], axis=0)
            yd = _dot(lhs, rhs)
            c0 = h0 * M2_HEADDIM
            ybuf[:, c0:c0 + 2 * M2_HEADDIM] = yd + yoff[:, pair * 2 * M2_HEADDIM:(pair + 1) * 2 * M2_HEADDIM]

    y = ybuf[...] + xs * dexp_ref[...]
    zf = z_ref[...].astype(_F32)
    v = y * (zf * _sigmoid(zf))
    out_ref[...] = _rms_f32(v, nw_ref[...]).astype(_BF16)


def _mamba(xbc, z, dt, cw, cb, dtb, a_neg, dexp, nw, e, bsz, seq):
    q = M2_CHUNK
    nc = seq // q
    tok = bsz * seq
    row = lambda b, c: (b * nc + c, 0)
    return pl.pallas_call(
        _mamba_kernel,
        grid=(bsz, nc),
        in_specs=[
            pl.BlockSpec((q, D_XBC), row),
            pl.BlockSpec((q, D_M2), row),
            pl.BlockSpec((q, DT_PAD), row),
            _resident((M2_CONV, D_XBC)),
            _resident((1, D_XBC)),
            _resident((1, DT_PAD)),
            _resident((1, DT_PAD)),
            _resident((1, D_M2)),
            _resident((1, D_M2)),
            _resident((DT_PAD, D_M2)),
        ],
        out_specs=pl.BlockSpec((q, D_M2), row),
        out_shape=jax.ShapeDtypeStruct((tok, D_M2), _BF16),
        scratch_shapes=[
            pltpu.VMEM((q + 8, D_XBC), _F32),
            pltpu.VMEM((M2_GROUPS, M2_STATE, M2_HEADS_PER_GROUP * M2_HEADDIM), _F32),
            pltpu.VMEM((q, D_M2), _F32),
        ],
        compiler_params=pltpu.CompilerParams(
            dimension_semantics=("arbitrary", "arbitrary"), vmem_limit_bytes=_VMEM_LIMIT),
        name="mamba",
    )(xbc, z, dt, cw, cb, dtb, a_neg, dexp, nw, e)


def _kv_kernel(mem_ref, g_ref, w_ref, kv_ref):
    mn = _rms_f32(mem_ref[...], g_ref[...]).astype(_BF16)
    kv_ref[...] = _dot(mn, w_ref[...]).astype(_BF16)


def _kv(mem2, g, w_kv, bsz):
    return pl.pallas_call(
        _kv_kernel,
        grid=(bsz,),
        in_specs=[
            pl.BlockSpec((MEM_LEN, D_MODEL), lambda b: (b, 0)),
            _resident((1, D_MODEL)),
            _resident((D_MODEL, 2 * D_XA)),
        ],
        out_specs=pl.BlockSpec((MEM_LEN, 2 * D_XA), lambda b: (b, 0)),
        out_shape=jax.ShapeDtypeStruct((bsz * MEM_LEN, 2 * D_XA), _BF16),
        compiler_params=pltpu.CompilerParams(
            dimension_semantics=("arbitrary",), vmem_limit_bytes=_VMEM_LIMIT),
        name="kv",
    )(mem2, g, w_kv)


def _attn_kernel(q_ref, kv_ref, out_ref):
    scale = XA_HEADDIM ** -0.5
    for h in range(XA_HEADS):
        lo = h * XA_HEADDIM
        qh = q_ref[:, lo:lo + XA_HEADDIM]
        kh = kv_ref[:, lo:lo + XA_HEADDIM]
        vh = kv_ref[:, D_XA + lo:D_XA + lo + XA_HEADDIM]
        s = lax.dot_general(qh, kh, (((1,), (1,)), ((), ())), preferred_element_type=_F32) * scale
        m = jnp.max(s, axis=-1, keepdims=True)
        p = jnp.exp(s - m)
        p = p / jnp.sum(p, axis=-1, keepdims=True)
        out_ref[:, lo:lo + XA_HEADDIM] = _dot(p.astype(_BF16), vh).astype(_BF16)


def _attn(q, kv, bsz, seq, tm):
    nt = seq // tm
    row = lambda b, t: (b * nt + t, 0)
    return pl.pallas_call(
        _attn_kernel,
        grid=(bsz, nt),
        in_specs=[
            pl.BlockSpec((tm, D_XA), row),
            pl.BlockSpec((MEM_LEN, 2 * D_XA), lambda b, t: (b, 0)),
        ],
        out_specs=pl.BlockSpec((tm, D_XA), row),
        out_shape=jax.ShapeDtypeStruct((bsz * seq, D_XA), _BF16),
        compiler_params=pltpu.CompilerParams(
            dimension_semantics=("arbitrary", "arbitrary"), vmem_limit_bytes=_VMEM_LIMIT),
        name="attn",
    )(q, kv)


def _merge_kernel(x_ref, gs5_ref, gates_ref, nb_ref, at_ref, wav_ref, wag_ref, wb_ref, wc_ref, wout_ref,
                  nf_ref, x1_ref, h2_ref):
    g = gs5_ref[...]
    ya = _dot(g, wav_ref[...]) * _sigmoid(_dot(g, wag_ref[...]))
    m = gates_ref[:, 0:D_MODEL].astype(_F32) * ya
    m = m + gates_ref[:, D_MODEL:2 * D_MODEL].astype(_F32) * _dot(nb_ref[...], wb_ref[...])
    m = m + gates_ref[:, 2 * D_MODEL:].astype(_F32) * _dot(at_ref[...], wc_ref[...])
    x1 = x_ref[...] + _dot(m.astype(_BF16), wout_ref[...])
    x1_ref[...] = x1
    h2_ref[...] = _rms_f32(x1, nf_ref[...]).astype(_BF16)


def _merge(x2, gs5, gates, nb, at, wav, wag, wb, wc, wout, nf, bsz, seq, tm):
    nt = seq // tm
    tok = bsz * seq
    row = lambda b, t: (b * nt + t, 0)
    return pl.pallas_call(
        _merge_kernel,
        grid=(bsz, nt),
        in_specs=[
            pl.BlockSpec((tm, D_MODEL), row),
            pl.BlockSpec((tm, D_MODEL), row),
            pl.BlockSpec((tm, D_GATES), row),
            pl.BlockSpec((tm, D_M2), row),
            pl.BlockSpec((tm, D_XA), row),
            _resident((D_MODEL, D_MODEL)),
            _resident((D_MODEL, D_MODEL)),
            _resident((D_M2, D_MODEL)),
            _resident((D_XA, D_MODEL)),
            _resident((D_MODEL, D_MODEL)),
            _resident((1, D_MODEL)),
        ],
        out_specs=[pl.BlockSpec((tm, D_MODEL), row), pl.BlockSpec((tm, D_MODEL), row)],
        out_shape=[jax.ShapeDtypeStruct((tok, D_MODEL), _F32), jax.ShapeDtypeStruct((tok, D_MODEL), _BF16)],
        compiler_params=pltpu.CompilerParams(
            dimension_semantics=("arbitrary", "arbitrary"), vmem_limit_bytes=_VMEM_LIMIT),
        name="merge",
    )(x2, gs5, gates, nb, at, wav, wag, wb, wc, wout, nf)


def _ffn_kernel(h2_ref, halo_ref, x1_ref, wup_ref, cw_ref, cb_ref, wdown_ref, nfin_ref, out_ref, ua, ub,
                *, tm, chunk):
    hl = 16
    first = pl.program_id(1) == 0
    halo = jnp.where(first, jnp.zeros((hl, D_MODEL), _BF16), halo_ref[...])
    h2e = jnp.concatenate([halo, h2_ref[...]], axis=0)
    acc = jnp.zeros((tm, D_MODEL), _F32)
    for j in range(D_FF // chunk):
        ca, cg = j * chunk, D_FF + j * chunk
        ua[...] = _dot(h2e, wup_ref[:, ca:ca + chunk])
        ub[...] = _dot(h2e, wup_ref[:, cg:cg + chunk])
        va = cb_ref[:, ca:ca + chunk]
        vb = cb_ref[:, cg:cg + chunk]
        for k in range(FFN_CONV):
            off = hl - (FFN_CONV - 1) + k
            va = va + cw_ref[k:k + 1, ca:ca + chunk] * ua[pl.ds(off, tm), :]
            vb = vb + cw_ref[k:k + 1, cg:cg + chunk] * ub[pl.ds(off, tm), :]
        act = (va * _sigmoid(va) * vb).astype(_BF16)
        acc = acc + _dot(act, wdown_ref[ca:ca + chunk, :])
    out_ref[...] = _rms_f32(x1_ref[...] + acc, nfin_ref[...])


def _ffn(h2, x1, wup, cw, cb, wdown, nfin, bsz, seq, tm):
    nt = seq // tm
    tok = bsz * seq
    hl = 16
    chunk = 256
    row = lambda b, t: (b * nt + t, 0)
    halo_row = lambda b, t: (jnp.maximum((b * nt + t) * (tm // hl) - 1, 0), 0)
    kern = functools.partial(_ffn_kernel, tm=tm, chunk=chunk)
    return pl.pallas_call(
        kern,
        grid=(bsz, nt),
        in_specs=[
            pl.BlockSpec((tm, D_MODEL), row),
            pl.BlockSpec((hl, D_MODEL), halo_row),
            pl.BlockSpec((tm, D_MODEL), row),
            _resident((D_MODEL, 2 * D_FF)),
            _resident((FFN_CONV, 2 * D_FF)),
            _resident((1, 2 * D_FF)),
            _resident((D_FF, D_MODEL)),
            _resident((1, D_MODEL)),
        ],
        out_specs=pl.BlockSpec((tm, D_MODEL), row),
        out_shape=jax.ShapeDtypeStruct((tok, D_MODEL), _F32),
        scratch_shapes=[pltpu.VMEM((hl + tm, chunk), _F32), pltpu.VMEM((hl + tm, chunk), _F32)],
        compiler_params=pltpu.CompilerParams(
            dimension_semantics=("arbitrary", "arbitrary"), vmem_limit_bytes=_VMEM_LIMIT),
        name="ffn",
    )(h2, h2, x1, wup, cw, cb, wdown, nfin)


def _s5_params(lam_re, lam_im, log_dt, b_re, b_im, c_re, c_im, d):
    dt = jnp.exp(log_dt)[:, None]
    mag = jnp.exp(lam_re * dt)
    ar = mag * jnp.cos(lam_im * dt)
    ai = mag * jnp.sin(lam_im * dt)
    den = lam_re * lam_re + lam_im * lam_im
    fr = ((ar - 1.0) * lam_re + ai * lam_im) / den
    fi = (ai * lam_re - (ar - 1.0) * lam_im) / den
    bbr = fr[..., None] * b_re - fi[..., None] * b_im
    bbi = fr[..., None] * b_im + fi[..., None] * b_re
    eye = jnp.eye(S5_GROUPS_PER_BLOCK, dtype=_F32)
    nb, gl, k, p = S5_BLOCKS, S5_GROUPS_PER_BLOCK, S5_CH_PER_GROUP, S5_STATE

    def bd_in(bb):
        t = bb.transpose(0, 2, 1).reshape(nb, gl, k, p)
        return jnp.einsum('jgkp,gh->jgkhp', t, eye).reshape(nb, gl * k, gl * p)

    def bd_out(c):
        t = c.transpose(0, 2, 1).reshape(nb, gl, p, k)
        return jnp.einsum('jgpk,gh->jgphk', t, eye).reshape(nb, gl * p, gl * k)

    bdb = jnp.concatenate([bd_in(bbr), bd_in(bbi)], axis=-1).astype(_BF16)
    bdc = jnp.concatenate([bd_out(c_re), -bd_out(c_im)], axis=1).astype(_BF16)
    a_re = ar.reshape(nb, 1, gl * p)
    a_im = ai.reshape(nb, 1, gl * p)
    d_blk = d.reshape(nb, 1, gl * k)
    return bdb, bdc, a_re, a_im, d_blk


def kernel(x, mem, norm_mix, w_in, s5_lambda_re, s5_lambda_im, s5_log_dt, s5_b_re, s5_b_im, s5_c_re, s5_c_im,
           s5_d, w_a_val, w_a_gate, m2_conv_w, m2_conv_b, m2_dt_bias, m2_a_log, m2_d, m2_norm, w_b, norm_mem,
           w_kv, w_c, w_out, norm_ffn, w_up, ffn_conv_w, ffn_conv_b, w_down, norm_final):
    bsz, seq, _ = x.shape
    depth = w_in.shape[0]
    tm = min(512, seq)
    tt = min(32, seq)
    p1 = D_MODEL
    p2 = p1 + D_M2
    p3 = p2 + D_XBC
    p4 = p3 + M2_HEADS
    p5 = p4 + D_XA
    row1 = lambda v: v.reshape(1, -1).astype(_F32)
    head_expand = (jnp.arange(DT_PAD)[:, None] == (jnp.arange(D_M2)[None, :] // M2_HEADDIM)).astype(_BF16)
    pad_heads = lambda v: jnp.pad(v.astype(_F32), (0, DT_PAD - M2_HEADS)).reshape(1, DT_PAD)

    assert depth == 1, "single-layer problem: the final RMSNorm is fused into the FFN kernel"
    l = 0
    x2 = x.reshape(bsz * seq, D_MODEL)
    mem2 = mem.reshape(bsz * MEM_LEN, D_MODEL)
    w = w_in[l]
    w_in_r = jnp.concatenate(
        [w[:, p2:p3], w[:, p5:], w[:, p1:p2], w[:, p4:p5], w[:, :p1],
         jnp.pad(w[:, p3:p4], ((0, 0), (0, DT_PAD - M2_HEADS)))], axis=1).astype(_BF16)
    xbc, gates, z, q, dt_raw, u = _inproj(x2, row1(norm_mix[l]), w_in_r, bsz, seq, tm)

    bdb, bdc, a_re, a_im, d_blk = _s5_params(
        s5_lambda_re[l], s5_lambda_im[l], s5_log_dt[l], s5_b_re[l], s5_b_im[l], s5_c_re[l], s5_c_im[l],
        s5_d[l])
    ridx = jnp.arange(tt * bsz)
    perm = (((ridx % bsz) * tt + ridx // bsz)[:, None] == ridx[None, :]).astype(_BF16)
    gs5 = _s5(u.reshape(bsz, seq, D_MODEL), perm, perm.T, bdb, bdc, a_re, a_im, d_blk, bsz, seq, tt)

    nb = _mamba(xbc, z, dt_raw, m2_conv_w[l].astype(_F32), row1(m2_conv_b[l]), pad_heads(m2_dt_bias[l]),
                pad_heads(-jnp.exp(m2_a_log[l].astype(_F32))),
                row1(jnp.repeat(m2_d[l], M2_HEADDIM)), row1(m2_norm[l]), head_expand, bsz, seq)

    kv = _kv(mem2, row1(norm_mem[l]), w_kv[l].astype(_BF16), bsz)
    at = _attn(q, kv, bsz, seq, tm)

    x1, h2 = _merge(x2, gs5.reshape(bsz * seq, D_MODEL), gates, nb, at,
                    w_a_val[l].astype(_BF16), w_a_gate[l].astype(_BF16), w_b[l].astype(_BF16),
                    w_c[l].astype(_BF16), w_out[l].astype(_BF16), row1(norm_ffn[l]), bsz, seq, tm)

    out = _ffn(h2, x1, w_up[l].astype(_BF16), ffn_conv_w[l].astype(_F32), row1(ffn_conv_b[l]),
               w_down[l].astype(_BF16), row1(norm_final), bsz, seq, tm)
    return out.reshape(bsz, seq, D_MODEL)
```

```python
import functools
import math

import jax
import jax.numpy as jnp
from jax import lax
from jax.experimental import pallas as pl
from jax.experimental.pallas import tpu as pltpu

D_MODEL = 1024
MEM_LEN = 256
S5_CH_PER_GROUP = 16
S5_GROUPS = 64
S5_STATE = 64
S5_GROUPS_PER_BLOCK = 16
S5_BLOCKS = S5_GROUPS // S5_GROUPS_PER_BLOCK
S5_BLOCK_CH = S5_GROUPS_PER_BLOCK * S5_CH_PER_GROUP
S5_BLOCK_STATE = S5_GROUPS_PER_BLOCK * S5_STATE
D_M2 = 2048
M2_HEADDIM = 64
M2_HEADS = 32
M2_GROUPS = 4
M2_HEADS_PER_GROUP = M2_HEADS // M2_GROUPS
M2_STATE = 128
M2_CONV = 4
M2_CHUNK = 128
D_BC = M2_GROUPS * M2_STATE
D_XBC = D_M2 + 2 * D_BC
XA_HEADS = 4
XA_HEADDIM = 128
D_XA = 512
D_FF = 2816
FFN_CONV = 3
D_GATES = 3 * D_MODEL
DT_PAD = 128
EPS = 1e-6

_O_XBC = 0
_O_GATES = _O_XBC + D_XBC
_O_Z = _O_GATES + D_GATES
_O_Q = _O_Z + D_M2
_O_U = _O_Q + D_XA
_O_DT = _O_U + D_MODEL
_W_IN_COLS = _O_DT + DT_PAD

_INPROJ_STEP = 256
_VMEM_LIMIT = 56 * 1024 * 1024
_F32 = jnp.float32
_BF16 = jnp.bfloat16


def _rms_f32(xf, g):
    inv = lax.rsqrt(jnp.mean(xf * xf, axis=-1, keepdims=True) + EPS)
    return xf * inv * g


def _sigmoid(x):
    return 0.5 * jnp.tanh(0.5 * x) + 0.5


def _dot(a, b):
    return jnp.dot(a, b, preferred_element_type=_F32)


def _resident(shape):
    nd = len(shape)
    return pl.BlockSpec(shape, lambda *_: (0,) * nd, pipeline_mode=pl.Buffered(1))


def _silu(x):
    hx = 0.5 * x
    return hx * (jnp.tanh(hx) + 1.0)


def _inproj_kernel(x_ref, g_ref, w_ref, cw_ref, cb_ref, xc_ref, gates_ref, zs_ref, q_ref, dt_ref, u_ref,
                   ext, halo, *, tm):
    h = _rms_f32(x_ref[...], g_ref[...]).astype(_BF16)
    step = _INPROJ_STEP
    hr = 8
    rb = 32

    @pl.when(pl.program_id(1) == 0)
    def _():
        halo[...] = jnp.zeros_like(halo)

    def conv_epilogue(c, slot):
        lw = 256
        for l0 in range(0, step, lw):
            taps = [cw_ref[k:k + 1, c + l0:c + l0 + lw] for k in range(M2_CONV)]
            bias = cb_ref[:, c + l0:c + l0 + lw]
            for r0 in range(0, tm, rb):
                win = ext[slot, r0:r0 + hr + rb, l0:l0 + lw]
                acc = bias + taps[M2_CONV - 1] * win[hr:]
                for k in range(1, M2_CONV):
                    acc = acc + taps[M2_CONV - 1 - k] * pltpu.roll(win, k, 0)[hr:]
                xc_ref[r0:r0 + rb, c + l0:c + l0 + lw] = _silu(acc).astype(_BF16)

    def plain_epilogue(r, out_ref, c, act, dtype):
        out_ref[:, c:c + r.shape[1]] = (r if act is None else act(r)).astype(dtype)

    convs = [(_O_XBC + c, step, None) for c in range(0, D_XBC, step)]
    plains = []
    for lo, width, out_ref, act, dtype in (
            (_O_GATES, D_GATES, gates_ref, _sigmoid, _BF16), (_O_Z, D_M2, zs_ref, _silu, _BF16),
            (_O_Q, D_XA, q_ref, None, _BF16), (_O_U, D_MODEL, u_ref, None, _BF16),
            (_O_DT, DT_PAD, dt_ref, None, _F32)):
        plains += [(lo + c, min(step, width - c), (out_ref, c, act, dtype)) for c in range(0, width, step)]
    items = []
    for i in range(max(len(convs), len(plains))):
        items += convs[i:i + 1] + plains[2 * i:2 * i + 2]
    items += plains[2 * max(len(convs), len(plains)):]

    pending = None
    for i, (lo, width, sink) in enumerate(items):
        r = _dot(h, w_ref[:, lo:lo + width])
        if sink is None:
            c, slot = lo - _O_XBC, i % 2
            ext[slot, 0:hr, :] = halo[:, c:c + step]
            ext[slot, hr:hr + tm, :] = r
            halo[:, c:c + step] = r[tm - hr:tm]
            epilogue = functools.partial(conv_epilogue, c, slot)
        else:
            epilogue = functools.partial(plain_epilogue, r, *sink)
        if pending is not None:
            pending()
        pending = epilogue
    pending()


def _inproj(x2, norm_g, w_in_r, cw, cb, bsz, seq, tm):
    nt = seq // tm
    tok = bsz * seq
    row = lambda b, t: (b * nt + t, 0)
    return pl.pallas_call(
        functools.partial(_inproj_kernel, tm=tm),
        grid=(bsz, nt),
        in_specs=[
            pl.BlockSpec((tm, D_MODEL), row),
            _resident((1, D_MODEL)),
            _resident((D_MODEL, _W_IN_COLS)),
            _resident((M2_CONV, D_XBC)),
            _resident((1, D_XBC)),
        ],
        out_specs=[
            pl.BlockSpec((tm, D_XBC), row),
            pl.BlockSpec((tm, D_GATES), row),
            pl.BlockSpec((tm, D_M2), row),
            pl.BlockSpec((tm, D_XA), row),
            pl.BlockSpec((tm, DT_PAD), row),
            pl.BlockSpec((tm, D_MODEL), row),
        ],
        out_shape=[
            jax.ShapeDtypeStruct((tok, D_XBC), _BF16),
            jax.ShapeDtypeStruct((tok, D_GATES), _BF16),
            jax.ShapeDtypeStruct((tok, D_M2), _BF16),
            jax.ShapeDtypeStruct((tok, D_XA), _BF16),
            jax.ShapeDtypeStruct((tok, DT_PAD), _F32),
            jax.ShapeDtypeStruct((tok, D_MODEL), _BF16),
        ],
        scratch_shapes=[pltpu.VMEM((2, 8 + tm, _INPROJ_STEP), _F32), pltpu.VMEM((8, D_XBC), _F32)],
        compiler_params=pltpu.CompilerParams(
            dimension_semantics=("arbitrary", "arbitrary"), vmem_limit_bytes=_VMEM_LIMIT),
        name="inproj",
    )(x2, norm_g, w_in_r, cw, cb)


def _gelu_tanh(x):
    c = math.sqrt(2.0 / math.pi)
    return 0.5 * x * (1.0 + jnp.tanh(c * (x + 0.044715 * (x * x * x))))


def _s5_kernel(u_ref, unext_ref, perm_ref, permt_ref, bdb_ref, bdc_ref, are_ref, aim_ref, d_ref, out_ref,
               buf0, buf1, xb0, xb1, ub0, ub1, state, *, bsz, tt, lane_chunk):
    ns = S5_BLOCK_STATE
    rows = bsz * tt
    half = rows // 2

    def project_in(u_bt, buf, ub):
        u2 = u_bt.reshape(rows, S5_BLOCK_CH)
        ub[0:half, :] = _dot(perm_ref[0:half, :], u2)
        ub[half:rows, :] = _dot(perm_ref[half:rows, :], u2)
        u_b = ub[...].astype(_BF16)
        buf[:, 0:ns] = _dot(u_b, bdb_ref[:, 0:ns])
        buf[:, ns:2 * ns] = _dot(u_b, bdb_ref[:, ns:2 * ns])

    def scan(buf, xb):
        for c in range(0, ns, lane_chunk):
            ar = jnp.broadcast_to(are_ref[:, c:c + lane_chunk], (bsz, lane_chunk))
            ai = jnp.broadcast_to(aim_ref[:, c:c + lane_chunk], (bsz, lane_chunk))
            xr = state[0, :, c:c + lane_chunk]
            xi = state[1, :, c:c + lane_chunk]
            for t in range(tt):
                r = slice(t * bsz, (t + 1) * bsz)
                nxr = ar * xr - ai * xi + buf[r, c:c + lane_chunk]
                nxi = ar * xi + ai * xr + buf[r, ns + c:ns + c + lane_chunk]
                xb[r, c:c + lane_chunk] = nxr.astype(_BF16)
                xb[r, ns + c:ns + c + lane_chunk] = nxi.astype(_BF16)
                xr, xi = nxr, nxi
            state[0, :, c:c + lane_chunk] = xr
            state[1, :, c:c + lane_chunk] = xi

    def project_out(xb, ub, t0):
        y = _dot(xb[:, 0:ns], bdc_ref[0:ns, :])
        y = y + _dot(xb[:, ns:2 * ns], bdc_ref[ns:2 * ns, :])
        y = y + d_ref[...] * ub[...]
        g_tb = _gelu_tanh(y).astype(_BF16)
        hb = bsz // 2
        top = _dot(permt_ref[0:half, :], g_tb).astype(_BF16)
        bot = _dot(permt_ref[half:rows, :], g_tb).astype(_BF16)
        out_ref[0:hb, t0:t0 + tt, :] = top.reshape(hb, tt, S5_BLOCK_CH)
        out_ref[hb:bsz, t0:t0 + tt, :] = bot.reshape(hb, tt, S5_BLOCK_CH)

    @pl.when(pl.program_id(1) == 0)
    def _():
        state[...] = jnp.zeros_like(state)
        project_in(u_ref[:, 0:tt, :], buf0, ub0)

    project_in(u_ref[:, tt:2 * tt, :], buf1, ub1)
    scan(buf0, xb0)
    project_out(xb0, ub0, 0)
    project_in(unext_ref[...], buf0, ub0)
    scan(buf1, xb1)
    project_out(xb1, ub1, tt)


def _s5(u, perm, permt, bdb, bdc, a_re, a_im, d_blk, bsz, seq, tt):
    rows = tt * bsz
    nsteps = seq // (2 * tt)
    last = seq // tt - 1
    kern = functools.partial(_s5_kernel, bsz=bsz, tt=tt, lane_chunk=512)
    blk = lambda g, j: (g, 0, 0)
    return pl.pallas_call(
        kern,
        grid=(S5_BLOCKS, nsteps),
        in_specs=[
            pl.BlockSpec((bsz, 2 * tt, S5_BLOCK_CH), lambda g, j: (0, j, g)),
            pl.BlockSpec((bsz, tt, S5_BLOCK_CH), lambda g, j: (0, jnp.minimum(2 * j + 2, last), g)),
            _resident((rows, rows)),
            _resident((rows, rows)),
            pl.BlockSpec((None, S5_BLOCK_CH, 2 * S5_BLOCK_STATE), blk),
            pl.BlockSpec((None, 2 * S5_BLOCK_STATE, S5_BLOCK_CH), blk),
            pl.BlockSpec((None, 1, S5_BLOCK_STATE), blk),
            pl.BlockSpec((None, 1, S5_BLOCK_STATE), blk),
            pl.BlockSpec((None, 1, S5_BLOCK_CH), blk),
        ],
        out_specs=pl.BlockSpec((bsz, 2 * tt, S5_BLOCK_CH), lambda g, j: (0, j, g)),
        out_shape=jax.ShapeDtypeStruct((bsz, seq, D_MODEL), _BF16),
        scratch_shapes=[
            pltpu.VMEM((rows, 2 * S5_BLOCK_STATE), _F32),
            pltpu.VMEM((rows, 2 * S5_BLOCK_STATE), _F32),
            pltpu.VMEM((rows, 2 * S5_BLOCK_STATE), _BF16),
            pltpu.VMEM((rows, 2 * S5_BLOCK_STATE), _BF16),
            pltpu.VMEM((rows, S5_BLOCK_CH), _F32),
            pltpu.VMEM((rows, S5_BLOCK_CH), _F32),
            pltpu.VMEM((2, bsz, S5_BLOCK_STATE), _F32),
        ],
        compiler_params=pltpu.CompilerParams(
            dimension_semantics=("arbitrary", "arbitrary"), vmem_limit_bytes=_VMEM_LIMIT),
        name="s5",
    )(u, u, perm, permt, bdb, bdc, a_re, a_im, d_blk)


def _softplus(x):
    return jnp.maximum(x, 0.0) + jnp.log(1.0 + jnp.exp(-jnp.abs(x)))


def _mamba_kernel(xc_ref, zs_ref, dt_ref, dtb_ref, a_ref, dexp_ref, nw_ref, e2_ref, out_ref, hst, ybuf):
    q = M2_CHUNK

    @pl.when(pl.program_id(1) == 0)
    def _():
        hst[...] = jnp.zeros_like(hst)

    xs = xc_ref[:, 0:D_M2].astype(_F32)
    bm = xc_ref[:, D_M2:D_M2 + D_BC]
    cm = xc_ref[:, D_M2 + D_BC:D_XBC]

    dt = _softplus(dt_ref[...] + dtb_ref[...])
    da = dt * a_ref[...]
    rowi = lax.broadcasted_iota(jnp.int32, (q, q), 0)
    coli = lax.broadcasted_iota(jnp.int32, (q, q), 1)
    causal = rowi >= coli
    tril = jnp.where(causal, 1.0, 0.0).astype(_F32)
    a_cum = jnp.dot(tril, da, preferred_element_type=_F32, precision=lax.Precision.HIGHEST)
    a_last = a_cum[q - 1:q, :]
    a_cum_t = a_cum.T

    def hilo(v):
        hi = v.astype(_BF16)
        lo = (v - hi.astype(_F32)).astype(_BF16)
        return jnp.concatenate([hi, lo], axis=1)

    pad = 16
    stacked = jnp.concatenate(
        [hilo(dt), hilo(jnp.exp(a_cum)), hilo(jnp.exp(a_last - a_cum)),
         hilo(jnp.broadcast_to(jnp.exp(a_last), (pad, DT_PAD)))], axis=0)
    ex = _dot(stacked, e2_ref[...])
    dt_e = ex[0:q]
    expa_e = ex[q:2 * q]
    dec_e = ex[2 * q:3 * q]
    cd_e = ex[3 * q:3 * q + 1]

    xdt = xs * dt_e
    xdt_b = xdt.astype(_BF16)
    xdec_b = (xdt * dec_e).astype(_BF16)
    lane = lax.broadcasted_iota(jnp.int32, (q, 2 * M2_HEADDIM), 1)
    gw = M2_HEADS_PER_GROUP * M2_HEADDIM

    for g in range(M2_GROUPS):
        cg = cm[:, g * M2_STATE:(g + 1) * M2_STATE]
        bg = bm[:, g * M2_STATE:(g + 1) * M2_STATE]
        cb = lax.dot_general(cg, bg, (((1,), (1,)), ((), ())), preferred_element_type=_F32)
        hprev = hst[g]
        yoff = _dot(cg, hprev.astype(_BF16)) * expa_e[:, g * gw:(g + 1) * gw]
        st = lax.dot_general(bg, xdec_b[:, g * gw:(g + 1) * gw], (((0,), (0,)), ((), ())),
                             preferred_element_type=_F32)
        hst[g] = hprev * cd_e[:, g * gw:(g + 1) * gw] + st
        for pair in range(M2_HEADS_PER_GROUP // 2):
            h0 = g * M2_HEADS_PER_GROUP + 2 * pair
            ms = []
            for hh in (h0, h0 + 1):
                seg = a_cum[:, hh:hh + 1] - a_cum_t[hh:hh + 1, :]
                lm = jnp.exp(jnp.where(causal, seg, -1e30))
                ms.append((cb * lm).astype(_BF16))
            lhs = jnp.concatenate(ms, axis=1)
            xp = xdt_b[:, h0 * M2_HEADDIM:(h0 + 2) * M2_HEADDIM]
            zero = jnp.zeros_like(xp)
            rhs = jnp.concatenate([jnp.where(lane < M2_HEADDIM, xp, zero),
                                   jnp.where(lane >= M2_HEADDIM, xp, zero)], axis=0)
            yd = _dot(lhs, rhs)
            c0 = h0 * M2_HEADDIM
            ybuf[:, c0:c0 + 2 * M2_HEADDIM] = yd + yoff[:, pair * 2 * M2_HEADDIM:(pair + 1) * 2 * M2_HEADDIM]

    y = ybuf[...] + xs * dexp_ref[...]
    v = y * zs_ref[...].astype(_F32)
    out_ref[...] = _rms_f32(v, nw_ref[...]).astype(_BF16)


def _mamba(xc, zs, dt, dtb, a_neg, dexp, nw, e2, bsz, seq):
    q = M2_CHUNK
    nc = seq // q
    tok = bsz * seq
    row = lambda b, c: (b * nc + c, 0)
    return pl.pallas_call(
        _mamba_kernel,
        grid=(bsz, nc),
        in_specs=[
            pl.BlockSpec((q, D_XBC), row),
            pl.BlockSpec((q, D_M2), row),
            pl.BlockSpec((q, DT_PAD), row),
            _resident((1, DT_PAD)),
            _resident((1, DT_PAD)),
            _resident((1, D_M2)),
            _resident((1, D_M2)),
            _resident((2 * DT_PAD, D_M2)),
        ],
        out_specs=pl.BlockSpec((q, D_M2), row),
        out_shape=jax.ShapeDtypeStruct((tok, D_M2), _BF16),
        scratch_shapes=[
            pltpu.VMEM((M2_GROUPS, M2_STATE, M2_HEADS_PER_GROUP * M2_HEADDIM), _F32),
            pltpu.VMEM((q, D_M2), _F32),
        ],
        compiler_params=pltpu.CompilerParams(
            dimension_semantics=("arbitrary", "arbitrary"), vmem_limit_bytes=_VMEM_LIMIT),
        name="mamba",
    )(xc, zs, dt, dtb, a_neg, dexp, nw, e2)


def _kv_kernel(mem_ref, g_ref, w_ref, kv_ref):
    mn = _rms_f32(mem_ref[...], g_ref[...]).astype(_BF16)
    kv_ref[...] = _dot(mn, w_ref[...]).astype(_BF16)


def _kv(mem2, g, w_kv, bsz):
    return pl.pallas_call(
        _kv_kernel,
        grid=(bsz,),
        in_specs=[
            pl.BlockSpec((MEM_LEN, D_MODEL), lambda b: (b, 0)),
            _resident((1, D_MODEL)),
            _resident((D_MODEL, 2 * D_XA)),
        ],
        out_specs=pl.BlockSpec((MEM_LEN, 2 * D_XA), lambda b: (b, 0)),
        out_shape=jax.ShapeDtypeStruct((bsz * MEM_LEN, 2 * D_XA), _BF16),
        compiler_params=pltpu.CompilerParams(
            dimension_semantics=("arbitrary",), vmem_limit_bytes=_VMEM_LIMIT),
        name="kv",
    )(mem2, g, w_kv)


def _attn_kernel(q_ref, kv_ref, out_ref):
    scale = XA_HEADDIM ** -0.5
    for h in range(XA_HEADS):
        lo = h * XA_HEADDIM
        qh = q_ref[:, lo:lo + XA_HEADDIM]
        kh = kv_ref[:, lo:lo + XA_HEADDIM]
        vh = kv_ref[:, D_XA + lo:D_XA + lo + XA_HEADDIM]
        s = lax.dot_general(qh, kh, (((1,), (1,)), ((), ())), preferred_element_type=_F32) * scale
        m = jnp.max(s, axis=-1, keepdims=True)
        p = jnp.exp(s - m)
        p = p / jnp.sum(p, axis=-1, keepdims=True)
        out_ref[:, lo:lo + XA_HEADDIM] = _dot(p.astype(_BF16), vh).astype(_BF16)


def _attn(q, kv, bsz, seq, tm):
    nt = seq // tm
    row = lambda b, t: (b * nt + t, 0)
    return pl.pallas_call(
        _attn_kernel,
        grid=(bsz, nt),
        in_specs=[
            pl.BlockSpec((tm, D_XA), row),
            pl.BlockSpec((MEM_LEN, 2 * D_XA), lambda b, t: (b, 0)),
        ],
        out_specs=pl.BlockSpec((tm, D_XA), row),
        out_shape=jax.ShapeDtypeStruct((bsz * seq, D_XA), _BF16),
        compiler_params=pltpu.CompilerParams(
            dimension_semantics=("arbitrary", "arbitrary"), vmem_limit_bytes=_VMEM_LIMIT),
        name="attn",
    )(q, kv)


def _merge_kernel(x_ref, gs5_ref, gates_ref, nb_ref, at_ref, wav_ref, wag_ref, wb_ref, wc_ref, wout_ref,
                  nf_ref, x1_ref, h2_ref):
    g = gs5_ref[...]
    ya = _dot(g, wav_ref[...]) * _sigmoid(_dot(g, wag_ref[...]))
    m = gates_ref[:, 0:D_MODEL].astype(_F32) * ya
    m = m + gates_ref[:, D_MODEL:2 * D_MODEL].astype(_F32) * _dot(nb_ref[...], wb_ref[...])
    m = m + gates_ref[:, 2 * D_MODEL:].astype(_F32) * _dot(at_ref[...], wc_ref[...])
    x1 = x_ref[...] + _dot(m.astype(_BF16), wout_ref[...])
    x1_ref[...] = x1
    h2_ref[...] = _rms_f32(x1, nf_ref[...]).astype(_BF16)


def _merge(x2, gs5, gates, nb, at, wav, wag, wb, wc, wout, nf, bsz, seq, tm):
    nt = seq // tm
    tok = bsz * seq
    row = lambda b, t: (b * nt + t, 0)
    return pl.pallas_call(
        _merge_kernel,
        grid=(bsz, nt),
        in_specs=[
            pl.BlockSpec((tm, D_MODEL), row),
            pl.BlockSpec((tm, D_MODEL), row),
            pl.BlockSpec((tm, D_GATES), row),
            pl.BlockSpec((tm, D_M2), row),
            pl.BlockSpec((tm, D_XA), row),
            _resident((D_MODEL, D_MODEL)),
            _resident((D_MODEL, D_MODEL)),
            _resident((D_M2, D_MODEL)),
            _resident((D_XA, D_MODEL)),
            _resident((D_MODEL, D_MODEL)),
            _resident((1, D_MODEL)),
        ],
        out_specs=[pl.BlockSpec((tm, D_MODEL), row), pl.BlockSpec((tm, D_MODEL), row)],
        out_shape=[jax.ShapeDtypeStruct((tok, D_MODEL), _F32), jax.ShapeDtypeStruct((tok, D_MODEL), _BF16)],
        compiler_params=pltpu.CompilerParams(
            dimension_semantics=("arbitrary", "arbitrary"), vmem_limit_bytes=_VMEM_LIMIT),
        name="merge",
    )(x2, gs5, gates, nb, at, wav, wag, wb, wc, wout, nf)


def _ffn_kernel(h2_ref, halo_ref, x1_ref, wup_ref, cw_ref, cb_ref, wdown_ref, nfin_ref, out_ref, *, tm, chunk):
    hl = 16
    first = pl.program_id(1) == 0
    halo = jnp.where(first, jnp.zeros((hl, D_MODEL), _BF16), halo_ref[...])
    h2e = jnp.concatenate([halo, h2_ref[...]], axis=0)

    def conv(u, c0):
        v = cb_ref[:, c0:c0 + chunk] + cw_ref[FFN_CONV - 1:FFN_CONV, c0:c0 + chunk] * u[hl:hl + tm]
        for k in range(1, FFN_CONV):
            tap = cw_ref[FFN_CONV - 1 - k:FFN_CONV - k, c0:c0 + chunk]
            v = v + tap * pltpu.roll(u, k, 0)[hl:hl + tm]
        return v

    def up(j):
        ca, cg = j * chunk, D_FF + j * chunk
        return _dot(h2e, wup_ref[:, ca:ca + chunk]), _dot(h2e, wup_ref[:, cg:cg + chunk])

    n = D_FF // chunk
    acc = jnp.zeros((tm, D_MODEL), _F32)
    nxt = up(0)
    for j in range(n):
        ua, ub = nxt
        if j + 1 < n:
            nxt = up(j + 1)
        ca, cg = j * chunk, D_FF + j * chunk
        act = (_silu(conv(ua, ca)) * conv(ub, cg)).astype(_BF16)
        acc = acc + _dot(act, wdown_ref[ca:ca + chunk, :])
    out_ref[...] = _rms_f32(x1_ref[...] + acc, nfin_ref[...])


def _ffn(h2, x1, wup, cw, cb, wdown, nfin, bsz, seq, tm):
    nt = seq // tm
    tok = bsz * seq
    hl = 16
    chunk = 256
    row = lambda b, t: (b * nt + t, 0)
    halo_row = lambda b, t: (jnp.maximum((b * nt + t) * (tm // hl) - 1, 0), 0)
    kern = functools.partial(_ffn_kernel, tm=tm, chunk=chunk)
    return pl.pallas_call(
        kern,
        grid=(bsz, nt),
        in_specs=[
            pl.BlockSpec((tm, D_MODEL), row),
            pl.BlockSpec((hl, D_MODEL), halo_row),
            pl.BlockSpec((tm, D_MODEL), row),
            _resident((D_MODEL, 2 * D_FF)),
            _resident((FFN_CONV, 2 * D_FF)),
            _resident((1, 2 * D_FF)),
            _resident((D_FF, D_MODEL)),
            _resident((1, D_MODEL)),
        ],
        out_specs=pl.BlockSpec((tm, D_MODEL), row),
        out_shape=jax.ShapeDtypeStruct((tok, D_MODEL), _F32),
        compiler_params=pltpu.CompilerParams(
            dimension_semantics=("arbitrary", "arbitrary"), vmem_limit_bytes=_VMEM_LIMIT),
        name="ffn",
    )(h2, h2, x1, wup, cw, cb, wdown, nfin)


def _s5_params(lam_re, lam_im, log_dt, b_re, b_im, c_re, c_im, d):
    dt = jnp.exp(log_dt)[:, None]
    mag = jnp.exp(lam_re * dt)
    ar = mag * jnp.cos(lam_im * dt)
    ai = mag * jnp.sin(lam_im * dt)
    den = lam_re * lam_re + lam_im * lam_im
    fr = ((ar - 1.0) * lam_re + ai * lam_im) / den
    fi = (ai * lam_re - (ar - 1.0) * lam_im) / den
    bbr = fr[..., None] * b_re - fi[..., None] * b_im
    bbi = fr[..., None] * b_im + fi[..., None] * b_re
    eye = jnp.eye(S5_GROUPS_PER_BLOCK, dtype=_F32)
    nb, gl, k, p = S5_BLOCKS, S5_GROUPS_PER_BLOCK, S5_CH_PER_GROUP, S5_STATE

    def bd_in(bb):
        t = bb.transpose(0, 2, 1).reshape(nb, gl, k, p)
        return jnp.einsum('jgkp,gh->jgkhp', t, eye).reshape(nb, gl * k, gl * p)

    def bd_out(c):
        t = c.transpose(0, 2, 1).reshape(nb, gl, p, k)
        return jnp.einsum('jgpk,gh->jgphk', t, eye).reshape(nb, gl * p, gl * k)

    bdb = jnp.concatenate([bd_in(bbr), bd_in(bbi)], axis=-1).astype(_BF16)
    bdc = jnp.concatenate([bd_out(c_re), -bd_out(c_im)], axis=1).astype(_BF16)
    a_re = ar.reshape(nb, 1, gl * p)
    a_im = ai.reshape(nb, 1, gl * p)
    d_blk = d.reshape(nb, 1, gl * k)
    return bdb, bdc, a_re, a_im, d_blk


def kernel(x, mem, norm_mix, w_in, s5_lambda_re, s5_lambda_im, s5_log_dt, s5_b_re, s5_b_im, s5_c_re, s5_c_im,
           s5_d, w_a_val, w_a_gate, m2_conv_w, m2_conv_b, m2_dt_bias, m2_a_log, m2_d, m2_norm, w_b, norm_mem,
           w_kv, w_c, w_out, norm_ffn, w_up, ffn_conv_w, ffn_conv_b, w_down, norm_final):
    bsz, seq, _ = x.shape
    depth = w_in.shape[0]
    tm = min(512, seq)
    tt = min(32, seq)
    p1 = D_MODEL
    p2 = p1 + D_M2
    p3 = p2 + D_XBC
    p4 = p3 + M2_HEADS
    p5 = p4 + D_XA
    row1 = lambda v: v.reshape(1, -1).astype(_F32)
    head_expand = (jnp.arange(DT_PAD)[:, None] == (jnp.arange(D_M2)[None, :] // M2_HEADDIM)).astype(_BF16)
    pad_heads = lambda v: jnp.pad(v.astype(_F32), (0, DT_PAD - M2_HEADS)).reshape(1, DT_PAD)

    assert depth == 1, "single-layer problem: the final RMSNorm is fused into the FFN kernel"
    l = 0
    x2 = x.reshape(bsz * seq, D_MODEL)
    mem2 = mem.reshape(bsz * MEM_LEN, D_MODEL)
    w = w_in[l]
    w_in_r = jnp.concatenate(
        [w[:, p2:p3], w[:, p5:], w[:, p1:p2], w[:, p4:p5], w[:, :p1],
         jnp.pad(w[:, p3:p4], ((0, 0), (0, DT_PAD - M2_HEADS)))], axis=1).astype(_BF16)
    xc, gates, zs, q, dt_raw, u = _inproj(x2, row1(norm_mix[l]), w_in_r, m2_conv_w[l].astype(_F32),
                                          row1(m2_conv_b[l]), bsz, seq, tm)

    bdb, bdc, a_re, a_im, d_blk = _s5_params(
        s5_lambda_re[l], s5_lambda_im[l], s5_log_dt[l], s5_b_re[l], s5_b_im[l], s5_c_re[l], s5_c_im[l],
        s5_d[l])
    ridx = jnp.arange(tt * bsz)
    perm = (((ridx % bsz) * tt + ridx // bsz)[:, None] == ridx[None, :]).astype(_BF16)
    gs5 = _s5(u.reshape(bsz, seq, D_MODEL), perm, perm.T, bdb, bdc, a_re, a_im, d_blk, bsz, seq, tt)

    nb = _mamba(xc, zs, dt_raw, pad_heads(m2_dt_bias[l]), pad_heads(-jnp.exp(m2_a_log[l].astype(_F32))),
                row1(jnp.repeat(m2_d[l], M2_HEADDIM)), row1(m2_norm[l]),
                jnp.concatenate([head_expand, head_expand], axis=0), bsz, seq)

    kv = _kv(mem2, row1(norm_mem[l]), w_kv[l].astype(_BF16), bsz)
    at = _attn(q, kv, bsz, seq, tm)

    x1, h2 = _merge(x2, gs5.reshape(bsz * seq, D_MODEL), gates, nb, at,
                    w_a_val[l].astype(_BF16), w_a_gate[l].astype(_BF16), w_b[l].astype(_BF16),
                    w_c[l].astype(_BF16), w_out[l].astype(_BF16), row1(norm_ffn[l]), bsz, seq, tm)

    out = _ffn(h2, x1, w_up[l].astype(_BF16), ffn_conv_w[l].astype(_F32), row1(ffn_conv_b[l]),
               w_down[l].astype(_BF16), row1(norm_final), bsz, seq, tm)
    return out.reshape(bsz, seq, D_MODEL)
```

```python
import functools
import math

import jax
import jax.numpy as jnp
from jax import lax
from jax.experimental import pallas as pl
from jax.experimental.pallas import tpu as pltpu

D_MODEL = 1024
MEM_LEN = 256
S5_CH_PER_GROUP = 16
S5_GROUPS = 64
S5_STATE = 64
S5_GROUPS_PER_BLOCK = 16
S5_BLOCKS = S5_GROUPS // S5_GROUPS_PER_BLOCK
S5_BLOCK_CH = S5_GROUPS_PER_BLOCK * S5_CH_PER_GROUP
S5_BLOCK_STATE = S5_GROUPS_PER_BLOCK * S5_STATE
D_M2 = 2048
M2_HEADDIM = 64
M2_HEADS = 32
M2_GROUPS = 4
M2_HEADS_PER_GROUP = M2_HEADS // M2_GROUPS
M2_STATE = 128
M2_CONV = 4
M2_CHUNK = 128
D_BC = M2_GROUPS * M2_STATE
D_XBC = D_M2 + 2 * D_BC
XA_HEADS = 4
XA_HEADDIM = 128
D_XA = 512
D_FF = 2816
FFN_CONV = 3
D_GATES = 3 * D_MODEL
DT_PAD = 128
EPS = 1e-6

_O_XBC = 0
_O_GATES = _O_XBC + D_XBC
_O_Z = _O_GATES + D_GATES
_O_Q = _O_Z + D_M2
_O_U = _O_Q + D_XA
_O_DT = _O_U + D_MODEL
_W_IN_COLS = _O_DT + DT_PAD

_INPROJ_STEP = 256
_VMEM_LIMIT = 56 * 1024 * 1024
_F32 = jnp.float32
_BF16 = jnp.bfloat16


def _rms_f32(xf, g):
    inv = lax.rsqrt(jnp.mean(xf * xf, axis=-1, keepdims=True) + EPS)
    return xf * inv * g


def _sigmoid(x):
    return 0.5 * jnp.tanh(0.5 * x) + 0.5


def _dot(a, b):
    return jnp.dot(a, b, preferred_element_type=_F32)


def _resident(shape):
    nd = len(shape)
    return pl.BlockSpec(shape, lambda *_: (0,) * nd, pipeline_mode=pl.Buffered(1))


def _silu(x):
    hx = 0.5 * x
    return hx * (jnp.tanh(hx) + 1.0)


def _inproj_kernel(x_ref, g_ref, w_ref, cw_ref, cb_ref, xc_ref, gates_ref, zs_ref, q_ref, dt_ref, u_ref,
                   ext, halo, *, tm):
    h = _rms_f32(x_ref[...], g_ref[...]).astype(_BF16)
    step = _INPROJ_STEP
    hr = 8
    rb = 32

    def proj(lo, c, width):
        return _dot(h, w_ref[:, lo + c:lo + c + width])

    @pl.when(pl.program_id(1) == 0)
    def _():
        halo[...] = jnp.zeros_like(halo)

    def conv_epilogue(c, slot):
        taps = [cw_ref[k:k + 1, c:c + step] for k in range(M2_CONV)]
        bias = cb_ref[:, c:c + step]
        for r0 in range(0, tm, rb):
            win = ext[slot, r0:r0 + hr + rb, :]
            acc = bias + taps[M2_CONV - 1] * win[hr:]
            for k in range(1, M2_CONV):
                acc = acc + taps[M2_CONV - 1 - k] * pltpu.roll(win, k, 0)[hr:]
            xc_ref[r0:r0 + rb, c:c + step] = _silu(acc).astype(_BF16)

    def plain_epilogue(r, out_ref, c, act, dtype):
        out_ref[:, c:c + r.shape[1]] = (r if act is None else act(r)).astype(dtype)

    convs = [(_O_XBC, c, step, None) for c in range(0, D_XBC, step)]
    plains = []
    for origin, width, out_ref, act, dtype in (
            (_O_GATES, D_GATES, gates_ref, _sigmoid, _BF16), (_O_Z, D_M2, zs_ref, _silu, _BF16),
            (_O_Q, D_XA, q_ref, None, _BF16), (_O_U, D_MODEL, u_ref, None, _BF16),
            (_O_DT, DT_PAD, dt_ref, None, _F32)):
        plains += [(origin, c, min(step, width - c), (out_ref, c, act, dtype)) for c in range(0, width, step)]
    items = []
    for i in range(max(len(convs), (len(plains) + 1) // 2)):
        items += convs[i:i + 1] + plains[2 * i:2 * i + 2]

    pending = None
    for i, (origin, c, width, sink) in enumerate(items):
        r = proj(origin, c, width)
        if sink is None:
            slot = i % 2
            ext[slot, 0:hr, :] = halo[:, c:c + step]
            ext[slot, hr:hr + tm, :] = r
            halo[:, c:c + step] = r[tm - hr:tm]
            epilogue = functools.partial(conv_epilogue, c, slot)
        else:
            epilogue = functools.partial(plain_epilogue, r, *sink)
        if pending is not None:
            pending()
        pending = epilogue
    pending()


def _inproj(x2, norm_g, w_in_r, cw, cb, bsz, seq, tm):
    nt = seq // tm
    tok = bsz * seq
    row = lambda b, t: (b * nt + t, 0)
    return pl.pallas_call(
        functools.partial(_inproj_kernel, tm=tm),
        grid=(bsz, nt),
        in_specs=[
            pl.BlockSpec((tm, D_MODEL), row),
            _resident((1, D_MODEL)),
            _resident((D_MODEL, _W_IN_COLS)),
            _resident((M2_CONV, D_XBC)),
            _resident((1, D_XBC)),
        ],
        out_specs=[
            pl.BlockSpec((tm, D_XBC), row),
            pl.BlockSpec((tm, D_GATES), row),
            pl.BlockSpec((tm, D_M2), row),
            pl.BlockSpec((tm, D_XA), row),
            pl.BlockSpec((tm, DT_PAD), row),
            pl.BlockSpec((tm, D_MODEL), row),
        ],
        out_shape=[
            jax.ShapeDtypeStruct((tok, D_XBC), _BF16),
            jax.ShapeDtypeStruct((tok, D_GATES), _BF16),
            jax.ShapeDtypeStruct((tok, D_M2), _BF16),
            jax.ShapeDtypeStruct((tok, D_XA), _BF16),
            jax.ShapeDtypeStruct((tok, DT_PAD), _F32),
            jax.ShapeDtypeStruct((tok, D_MODEL), _BF16),
        ],
        scratch_shapes=[pltpu.VMEM((2, 8 + tm, _INPROJ_STEP), _F32), pltpu.VMEM((8, D_XBC), _F32)],
        compiler_params=pltpu.CompilerParams(
            dimension_semantics=("arbitrary", "arbitrary"), vmem_limit_bytes=_VMEM_LIMIT),
        name="inproj",
    )(x2, norm_g, w_in_r, cw, cb)


def _gelu_tanh(x):
    c = math.sqrt(2.0 / math.pi)
    return 0.5 * x * (1.0 + jnp.tanh(c * (x + 0.044715 * (x * x * x))))


def _s5_kernel(u_ref, unext_ref, perm_ref, permt_ref, bdb_ref, bdc_ref, are_ref, aim_ref, d_ref, out_ref,
               buf0, buf1, xb0, xb1, ub0, ub1, state, *, bsz, tt, nchunks, lane_chunk):
    ns = S5_BLOCK_STATE
    rows = bsz * tt
    half = rows // 2

    def project_in(u_bt, buf, ub):
        u2 = u_bt.reshape(rows, S5_BLOCK_CH)
        ub[0:half, :] = _dot(perm_ref[0:half, :], u2)
        ub[half:rows, :] = _dot(perm_ref[half:rows, :], u2)
        u_b = ub[...].astype(_BF16)
        buf[:, 0:ns] = _dot(u_b, bdb_ref[:, 0:ns])
        buf[:, ns:2 * ns] = _dot(u_b, bdb_ref[:, ns:2 * ns])

    def scan(buf, xb):
        for c in range(0, ns, lane_chunk):
            ar = jnp.broadcast_to(are_ref[:, c:c + lane_chunk], (bsz, lane_chunk))
            ai = jnp.broadcast_to(aim_ref[:, c:c + lane_chunk], (bsz, lane_chunk))
            xr = state[0, :, c:c + lane_chunk]
            xi = state[1, :, c:c + lane_chunk]
            for t in range(tt):
                r = slice(t * bsz, (t + 1) * bsz)
                nxr = ar * xr - ai * xi + buf[r, c:c + lane_chunk]
                nxi = ar * xi + ai * xr + buf[r, ns + c:ns + c + lane_chunk]
                xb[r, c:c + lane_chunk] = nxr.astype(_BF16)
                xb[r, ns + c:ns + c + lane_chunk] = nxi.astype(_BF16)
                xr, xi = nxr, nxi
            state[0, :, c:c + lane_chunk] = xr
            state[1, :, c:c + lane_chunk] = xi

    def project_out(xb, ub, t0):
        y = _dot(xb[:, 0:ns], bdc_ref[0:ns, :])
        y = y + _dot(xb[:, ns:2 * ns], bdc_ref[ns:2 * ns, :])
        y_tb = (y + d_ref[...] * ub[...]).astype(_BF16)
        hb = bsz // 2
        top = _dot(permt_ref[0:half, :], y_tb).astype(_BF16)
        bot = _dot(permt_ref[half:rows, :], y_tb).astype(_BF16)
        out_ref[0:hb, t0:t0 + tt, :] = top.reshape(hb, tt, S5_BLOCK_CH)
        out_ref[hb:bsz, t0:t0 + tt, :] = bot.reshape(hb, tt, S5_BLOCK_CH)

    bufs, xbs, ubs = (buf0, buf1), (xb0, xb1), (ub0, ub1)

    @pl.when(pl.program_id(1) == 0)
    def _():
        state[...] = jnp.zeros_like(state)
        project_in(u_ref[:, 0:tt, :], buf0, ub0)

    for k in range(nchunks):
        nxt = (k + 1) % 2
        if k + 1 < nchunks:
            project_in(u_ref[:, (k + 1) * tt:(k + 2) * tt, :], bufs[nxt], ubs[nxt])
        else:
            project_in(unext_ref[...], bufs[nxt], ubs[nxt])
        scan(bufs[k % 2], xbs[k % 2])
        project_out(xbs[k % 2], ubs[k % 2], k * tt)


def _s5(u, perm, permt, bdb, bdc, a_re, a_im, d_blk, bsz, seq, tt, nchunks):
    rows = tt * bsz
    nsteps = seq // (nchunks * tt)
    last = seq // tt - 1
    kern = functools.partial(_s5_kernel, bsz=bsz, tt=tt, nchunks=nchunks, lane_chunk=512)
    blk = lambda g, j: (g, 0, 0)
    return pl.pallas_call(
        kern,
        grid=(S5_BLOCKS, nsteps),
        in_specs=[
            pl.BlockSpec((bsz, nchunks * tt, S5_BLOCK_CH), lambda g, j: (0, j, g)),
            pl.BlockSpec((bsz, tt, S5_BLOCK_CH), lambda g, j: (0, jnp.minimum(nchunks * (j + 1), last), g)),
            _resident((rows, rows)),
            _resident((rows, rows)),
            pl.BlockSpec((None, S5_BLOCK_CH, 2 * S5_BLOCK_STATE), blk),
            pl.BlockSpec((None, 2 * S5_BLOCK_STATE, S5_BLOCK_CH), blk),
            pl.BlockSpec((None, 1, S5_BLOCK_STATE), blk),
            pl.BlockSpec((None, 1, S5_BLOCK_STATE), blk),
            pl.BlockSpec((None, 1, S5_BLOCK_CH), blk),
        ],
        out_specs=pl.BlockSpec((bsz, nchunks * tt, S5_BLOCK_CH), lambda g, j: (0, j, g)),
        out_shape=jax.ShapeDtypeStruct((bsz, seq, D_MODEL), _BF16),
        scratch_shapes=[
            pltpu.VMEM((rows, 2 * S5_BLOCK_STATE), _F32),
            pltpu.VMEM((rows, 2 * S5_BLOCK_STATE), _F32),
            pltpu.VMEM((rows, 2 * S5_BLOCK_STATE), _BF16),
            pltpu.VMEM((rows, 2 * S5_BLOCK_STATE), _BF16),
            pltpu.VMEM((rows, S5_BLOCK_CH), _F32),
            pltpu.VMEM((rows, S5_BLOCK_CH), _F32),
            pltpu.VMEM((2, bsz, S5_BLOCK_STATE), _F32),
        ],
        compiler_params=pltpu.CompilerParams(
            dimension_semantics=("arbitrary", "arbitrary"), vmem_limit_bytes=_VMEM_LIMIT),
        name="s5",
    )(u, u, perm, permt, bdb, bdc, a_re, a_im, d_blk)


def _softplus(x):
    return jnp.maximum(x, 0.0) + jnp.log(1.0 + jnp.exp(-jnp.abs(x)))


def _mamba_kernel(xc_ref, zs_ref, dt_ref, dtb_ref, a_ref, dexp_ref, nw_ref, e2_ref, out_ref, hst, ybuf, *, nsub):
    @pl.when(pl.program_id(1) == 0)
    def _():
        hst[...] = jnp.zeros_like(hst)

    for sub in range(nsub):
        _mamba_chunk(sub * M2_CHUNK, xc_ref, zs_ref, dt_ref, dtb_ref, a_ref, dexp_ref, nw_ref, e2_ref, out_ref,
                     hst, ybuf)


def _mamba_chunk(r0, xc_ref, zs_ref, dt_ref, dtb_ref, a_ref, dexp_ref, nw_ref, e2_ref, out_ref, hst, ybuf):
    q = M2_CHUNK
    rs = slice(r0, r0 + q)
    xs = xc_ref[rs, 0:D_M2].astype(_F32)
    bm = xc_ref[rs, D_M2:D_M2 + D_BC]
    cm = xc_ref[rs, D_M2 + D_BC:D_XBC]

    dt = _softplus(dt_ref[rs, :] + dtb_ref[...])
    da = dt * a_ref[...]
    rowi = lax.broadcasted_iota(jnp.int32, (q, q), 0)
    coli = lax.broadcasted_iota(jnp.int32, (q, q), 1)
    causal = rowi >= coli
    tril = jnp.where(causal, 1.0, 0.0).astype(_F32)
    a_cum = jnp.dot(tril, da, preferred_element_type=_F32, precision=lax.Precision.HIGHEST)
    a_last = a_cum[q - 1:q, :]
    a_cum_t = a_cum.T

    def hilo(v):
        hi = v.astype(_BF16)
        lo = (v - hi.astype(_F32)).astype(_BF16)
        return jnp.concatenate([hi, lo], axis=1)

    pad = 16
    stacked = jnp.concatenate(
        [hilo(dt), hilo(jnp.exp(a_cum)), hilo(jnp.exp(a_last - a_cum)),
         hilo(jnp.broadcast_to(jnp.exp(a_last), (pad, DT_PAD)))], axis=0)
    ex = _dot(stacked, e2_ref[...])
    dt_e = ex[0:q]
    expa_e = ex[q:2 * q]
    dec_e = ex[2 * q:3 * q]
    cd_e = ex[3 * q:3 * q + 1]

    xdt = xs * dt_e
    xdt_b = xdt.astype(_BF16)
    xdec_b = (xdt * dec_e).astype(_BF16)
    lane = lax.broadcasted_iota(jnp.int32, (q, 2 * M2_HEADDIM), 1)
    gw = M2_HEADS_PER_GROUP * M2_HEADDIM

    for g in range(M2_GROUPS):
        cg = cm[:, g * M2_STATE:(g + 1) * M2_STATE]
        bg = bm[:, g * M2_STATE:(g + 1) * M2_STATE]
        cb = lax.dot_general(cg, bg, (((1,), (1,)), ((), ())), preferred_element_type=_F32)
        hprev = hst[g]
        yoff = _dot(cg, hprev.astype(_BF16)) * expa_e[:, g * gw:(g + 1) * gw]
        st = lax.dot_general(bg, xdec_b[:, g * gw:(g + 1) * gw], (((0,), (0,)), ((), ())),
                             preferred_element_type=_F32)
        hst[g] = hprev * cd_e[:, g * gw:(g + 1) * gw] + st
        for pair in range(M2_HEADS_PER_GROUP // 2):
            h0 = g * M2_HEADS_PER_GROUP + 2 * pair
            ms = []
            for hh in (h0, h0 + 1):
                seg = a_cum[:, hh:hh + 1] - a_cum_t[hh:hh + 1, :]
                lm = jnp.exp(jnp.where(causal, seg, -1e30))
                ms.append((cb * lm).astype(_BF16))
            lhs = jnp.concatenate(ms, axis=1)
            xp = xdt_b[:, h0 * M2_HEADDIM:(h0 + 2) * M2_HEADDIM]
            zero = jnp.zeros_like(xp)
            rhs = jnp.concatenate([jnp.where(lane < M2_HEADDIM, xp, zero),
                                   jnp.where(lane >= M2_HEADDIM, xp, zero)], axis=0)
            yd = _dot(lhs, rhs)
            c0 = h0 * M2_HEADDIM
            ybuf[rs, c0:c0 + 2 * M2_HEADDIM] = yd + yoff[:, pair * 2 * M2_HEADDIM:(pair + 1) * 2 * M2_HEADDIM]

    y = ybuf[rs, :] + xs * dexp_ref[...]
    v = y * zs_ref[rs, :].astype(_F32)
    out_ref[rs, :] = _rms_f32(v, nw_ref[...]).astype(_BF16)


def _mamba(xc, zs, dt, dtb, a_neg, dexp, nw, e2, bsz, seq, nsub):
    q = nsub * M2_CHUNK
    nc = seq // q
    tok = bsz * seq
    row = lambda b, c: (b * nc + c, 0)
    return pl.pallas_call(
        functools.partial(_mamba_kernel, nsub=nsub),
        grid=(bsz, nc),
        in_specs=[
            pl.BlockSpec((q, D_XBC), row),
            pl.BlockSpec((q, D_M2), row),
            pl.BlockSpec((q, DT_PAD), row),
            _resident((1, DT_PAD)),
            _resident((1, DT_PAD)),
            _resident((1, D_M2)),
            _resident((1, D_M2)),
            _resident((2 * DT_PAD, D_M2)),
        ],
        out_specs=pl.BlockSpec((q, D_M2), row),
        out_shape=jax.ShapeDtypeStruct((tok, D_M2), _BF16),
        scratch_shapes=[
            pltpu.VMEM((M2_GROUPS, M2_STATE, M2_HEADS_PER_GROUP * M2_HEADDIM), _F32),
            pltpu.VMEM((q, D_M2), _F32),
        ],
        compiler_params=pltpu.CompilerParams(
            dimension_semantics=("arbitrary", "arbitrary"), vmem_limit_bytes=_VMEM_LIMIT),
        name="mamba",
    )(xc, zs, dt, dtb, a_neg, dexp, nw, e2)


def _kv_kernel(mem_ref, g_ref, w_ref, kv_ref):
    mn = _rms_f32(mem_ref[...], g_ref[...]).astype(_BF16)
    kv_ref[...] = _dot(mn, w_ref[...]).astype(_BF16)


def _kv(mem2, g, w_kv, bsz):
    return pl.pallas_call(
        _kv_kernel,
        grid=(bsz,),
        in_specs=[
            pl.BlockSpec((MEM_LEN, D_MODEL), lambda b: (b, 0)),
            _resident((1, D_MODEL)),
            _resident((D_MODEL, 2 * D_XA)),
        ],
        out_specs=pl.BlockSpec((MEM_LEN, 2 * D_XA), lambda b: (b, 0)),
        out_shape=jax.ShapeDtypeStruct((bsz * MEM_LEN, 2 * D_XA), _BF16),
        compiler_params=pltpu.CompilerParams(
            dimension_semantics=("arbitrary",), vmem_limit_bytes=_VMEM_LIMIT),
        name="kv",
    )(mem2, g, w_kv)


def _attn_kernel(q_ref, kv_ref, out_ref):
    scale = XA_HEADDIM ** -0.5
    for h in range(XA_HEADS):
        lo = h * XA_HEADDIM
        qh = q_ref[:, lo:lo + XA_HEADDIM]
        kh = kv_ref[:, lo:lo + XA_HEADDIM]
        vh = kv_ref[:, D_XA + lo:D_XA + lo + XA_HEADDIM]
        s = lax.dot_general(qh, kh, (((1,), (1,)), ((), ())), preferred_element_type=_F32) * scale
        m = jnp.max(s, axis=-1, keepdims=True)
        p = jnp.exp(s - m)
        p = p / jnp.sum(p, axis=-1, keepdims=True)
        out_ref[:, lo:lo + XA_HEADDIM] = _dot(p.astype(_BF16), vh).astype(_BF16)


def _attn(q, kv, bsz, seq, tm):
    nt = seq // tm
    row = lambda b, t: (b * nt + t, 0)
    return pl.pallas_call(
        _attn_kernel,
        grid=(bsz, nt),
        in_specs=[
            pl.BlockSpec((tm, D_XA), row),
            pl.BlockSpec((MEM_LEN, 2 * D_XA), lambda b, t: (b, 0)),
        ],
        out_specs=pl.BlockSpec((tm, D_XA), row),
        out_shape=jax.ShapeDtypeStruct((bsz * seq, D_XA), _BF16),
        compiler_params=pltpu.CompilerParams(
            dimension_semantics=("arbitrary", "arbitrary"), vmem_limit_bytes=_VMEM_LIMIT),
        name="attn",
    )(q, kv)


def _merge_kernel(x_ref, gs5_ref, gates_ref, nb_ref, at_ref, wav_ref, wag_ref, wb_ref, wc_ref, wout_ref,
                  nf_ref, x1_ref, h2_ref):
    g = _gelu_tanh(gs5_ref[...].astype(_F32)).astype(_BF16)
    ya = _dot(g, wav_ref[...]) * _sigmoid(_dot(g, wag_ref[...]))
    m = gates_ref[:, 0:D_MODEL].astype(_F32) * ya
    m = m + gates_ref[:, D_MODEL:2 * D_MODEL].astype(_F32) * _dot(nb_ref[...], wb_ref[...])
    m = m + gates_ref[:, 2 * D_MODEL:].astype(_F32) * _dot(at_ref[...], wc_ref[...])
    x1 = x_ref[...] + _dot(m.astype(_BF16), wout_ref[...])
    x1_ref[...] = x1
    h2_ref[...] = _rms_f32(x1, nf_ref[...]).astype(_BF16)


def _merge(x2, gs5, gates, nb, at, wav, wag, wb, wc, wout, nf, bsz, seq, tm):
    nt = seq // tm
    tok = bsz * seq
    row = lambda b, t: (b * nt + t, 0)
    return pl.pallas_call(
        _merge_kernel,
        grid=(bsz, nt),
        in_specs=[
            pl.BlockSpec((tm, D_MODEL), row),
            pl.BlockSpec((tm, D_MODEL), row),
            pl.BlockSpec((tm, D_GATES), row),
            pl.BlockSpec((tm, D_M2), row),
            pl.BlockSpec((tm, D_XA), row),
            _resident((D_MODEL, D_MODEL)),
            _resident((D_MODEL, D_MODEL)),
            _resident((D_M2, D_MODEL)),
            _resident((D_XA, D_MODEL)),
            _resident((D_MODEL, D_MODEL)),
            _resident((1, D_MODEL)),
        ],
        out_specs=[pl.BlockSpec((tm, D_MODEL), row), pl.BlockSpec((tm, D_MODEL), row)],
        out_shape=[jax.ShapeDtypeStruct((tok, D_MODEL), _F32), jax.ShapeDtypeStruct((tok, D_MODEL), _BF16)],
        compiler_params=pltpu.CompilerParams(
            dimension_semantics=("arbitrary", "arbitrary"), vmem_limit_bytes=_VMEM_LIMIT),
        name="merge",
    )(x2, gs5, gates, nb, at, wav, wag, wb, wc, wout, nf)


def _ffn_kernel(h2_ref, halo_ref, x1_ref, wup_ref, cw_ref, cb_ref, wdown_ref, nfin_ref, out_ref, act_s,
                *, tm, chunk):
    hl = 16
    first = pl.program_id(1) == 0
    halo = jnp.where(first, jnp.zeros((hl, D_MODEL), _BF16), halo_ref[...])
    h2e = jnp.concatenate([halo, h2_ref[...]], axis=0)

    def conv(u, c0):
        v = cb_ref[:, c0:c0 + chunk] + cw_ref[FFN_CONV - 1:FFN_CONV, c0:c0 + chunk] * u[hl:hl + tm]
        for k in range(1, FFN_CONV):
            tap = cw_ref[FFN_CONV - 1 - k:FFN_CONV - k, c0:c0 + chunk]
            v = v + tap * pltpu.roll(u, k, 0)[hl:hl + tm]
        return v

    def up(j):
        ca, cg = j * chunk, D_FF + j * chunk
        return _dot(h2e, wup_ref[:, ca:ca + chunk]), _dot(h2e, wup_ref[:, cg:cg + chunk])

    n = D_FF // chunk
    nxt = up(0)
    for j in range(n):
        ua, ub = nxt
        if j + 1 < n:
            nxt = up(j + 1)
        ca, cg = j * chunk, D_FF + j * chunk
        act_s[:, ca:ca + chunk] = (_silu(conv(ua, ca)) * conv(ub, cg)).astype(_BF16)
    down = _dot(act_s[...], wdown_ref[...])
    out_ref[...] = _rms_f32(x1_ref[...] + down, nfin_ref[...])


def _ffn(h2, x1, wup, cw, cb, wdown, nfin, bsz, seq, tm):
    nt = seq // tm
    tok = bsz * seq
    hl = 16
    chunk = 256
    row = lambda b, t: (b * nt + t, 0)
    halo_row = lambda b, t: (jnp.maximum((b * nt + t) * (tm // hl) - 1, 0), 0)
    kern = functools.partial(_ffn_kernel, tm=tm, chunk=chunk)
    return pl.pallas_call(
        kern,
        grid=(bsz, nt),
        in_specs=[
            pl.BlockSpec((tm, D_MODEL), row),
            pl.BlockSpec((hl, D_MODEL), halo_row),
            pl.BlockSpec((tm, D_MODEL), row),
            _resident((D_MODEL, 2 * D_FF)),
            _resident((FFN_CONV, 2 * D_FF)),
            _resident((1, 2 * D_FF)),
            _resident((D_FF, D_MODEL)),
            _resident((1, D_MODEL)),
        ],
        out_specs=pl.BlockSpec((tm, D_MODEL), row),
        out_shape=jax.ShapeDtypeStruct((tok, D_MODEL), _F32),
        scratch_shapes=[pltpu.VMEM((tm, D_FF), _BF16)],
        compiler_params=pltpu.CompilerParams(
            dimension_semantics=("arbitrary", "arbitrary"), vmem_limit_bytes=_VMEM_LIMIT),
        name="ffn",
    )(h2, h2, x1, wup, cw, cb, wdown, nfin)


def _s5_params(lam_re, lam_im, log_dt, b_re, b_im, c_re, c_im, d):
    dt = jnp.exp(log_dt)[:, None]
    mag = jnp.exp(lam_re * dt)
    ar = mag * jnp.cos(lam_im * dt)
    ai = mag * jnp.sin(lam_im * dt)
    den = lam_re * lam_re + lam_im * lam_im
    fr = ((ar - 1.0) * lam_re + ai * lam_im) / den
    fi = (ai * lam_re - (ar - 1.0) * lam_im) / den
    bbr = fr[..., None] * b_re - fi[..., None] * b_im
    bbi = fr[..., None] * b_im + fi[..., None] * b_re
    eye = jnp.eye(S5_GROUPS_PER_BLOCK, dtype=_F32)
    nb, gl, k, p = S5_BLOCKS, S5_GROUPS_PER_BLOCK, S5_CH_PER_GROUP, S5_STATE

    def bd_in(bb):
        t = bb.transpose(0, 2, 1).reshape(nb, gl, k, p)
        return jnp.einsum('jgkp,gh->jgkhp', t, eye).reshape(nb, gl * k, gl * p)

    def bd_out(c):
        t = c.transpose(0, 2, 1).reshape(nb, gl, p, k)
        return jnp.einsum('jgpk,gh->jgphk', t, eye).reshape(nb, gl * p, gl * k)

    bdb = jnp.concatenate([bd_in(bbr), bd_in(bbi)], axis=-1).astype(_BF16)
    bdc = jnp.concatenate([bd_out(c_re), -bd_out(c_im)], axis=1).astype(_BF16)
    a_re = ar.reshape(nb, 1, gl * p)
    a_im = ai.reshape(nb, 1, gl * p)
    d_blk = d.reshape(nb, 1, gl * k)
    return bdb, bdc, a_re, a_im, d_blk


def kernel(x, mem, norm_mix, w_in, s5_lambda_re, s5_lambda_im, s5_log_dt, s5_b_re, s5_b_im, s5_c_re, s5_c_im,
           s5_d, w_a_val, w_a_gate, m2_conv_w, m2_conv_b, m2_dt_bias, m2_a_log, m2_d, m2_norm, w_b, norm_mem,
           w_kv, w_c, w_out, norm_ffn, w_up, ffn_conv_w, ffn_conv_b, w_down, norm_final):
    bsz, seq, _ = x.shape
    depth = w_in.shape[0]
    tm = min(512, seq)
    tt = 16
    s5_chunks = 4
    p1 = D_MODEL
    p2 = p1 + D_M2
    p3 = p2 + D_XBC
    p4 = p3 + M2_HEADS
    p5 = p4 + D_XA
    row1 = lambda v: v.reshape(1, -1).astype(_F32)
    head_expand = (jnp.arange(DT_PAD)[:, None] == (jnp.arange(D_M2)[None, :] // M2_HEADDIM)).astype(_BF16)
    pad_heads = lambda v: jnp.pad(v.astype(_F32), (0, DT_PAD - M2_HEADS)).reshape(1, DT_PAD)

    assert depth == 1, "single-layer problem: the final RMSNorm is fused into the FFN kernel"
    l = 0
    x2 = x.reshape(bsz * seq, D_MODEL)
    mem2 = mem.reshape(bsz * MEM_LEN, D_MODEL)
    w = w_in[l]
    w_in_r = jnp.concatenate(
        [w[:, p2:p3], w[:, p5:], w[:, p1:p2], w[:, p4:p5], w[:, :p1],
         jnp.pad(w[:, p3:p4], ((0, 0), (0, DT_PAD - M2_HEADS)))], axis=1).astype(_BF16)
    xc, gates, zs, q, dt_raw, u = _inproj(x2, row1(norm_mix[l]), w_in_r, m2_conv_w[l].astype(_F32),
                                          row1(m2_conv_b[l]), bsz, seq, tm)

    bdb, bdc, a_re, a_im, d_blk = _s5_params(
        s5_lambda_re[l], s5_lambda_im[l], s5_log_dt[l], s5_b_re[l], s5_b_im[l], s5_c_re[l], s5_c_im[l],
        s5_d[l])
    ridx = jnp.arange(tt * bsz)
    perm = (((ridx % bsz) * tt + ridx // bsz)[:, None] == ridx[None, :]).astype(_BF16)
    gs5 = _s5(u.reshape(bsz, seq, D_MODEL), perm, perm.T, bdb, bdc, a_re, a_im, d_blk, bsz, seq, tt, s5_chunks)

    nb = _mamba(xc, zs, dt_raw, pad_heads(m2_dt_bias[l]), pad_heads(-jnp.exp(m2_a_log[l].astype(_F32))),
                row1(jnp.repeat(m2_d[l], M2_HEADDIM)), row1(m2_norm[l]),
                jnp.concatenate([head_expand, head_expand], axis=0), bsz, seq, min(4, seq // M2_CHUNK))

    kv = _kv(mem2, row1(norm_mem[l]), w_kv[l].astype(_BF16), bsz)
    at = _attn(q, kv, bsz, seq, tm)

    x1, h2 = _merge(x2, gs5.reshape(bsz * seq, D_MODEL), gates, nb, at,
                    w_a_val[l].astype(_BF16), w_a_gate[l].astype(_BF16), w_b[l].astype(_BF16),
                    w_c[l].astype(_BF16), w_out[l].astype(_BF16), row1(norm_ffn[l]), bsz, seq, tm)

    out = _ffn(h2, x1, w_up[l].astype(_BF16), ffn_conv_w[l].astype(_F32), row1(ffn_conv_b[l]),
               w_down[l].astype(_BF16), row1(norm_final), bsz, seq, tm)
    return out.reshape(bsz, seq, D_MODEL)
```

```python
import functools
import math

import jax
import jax.numpy as jnp
from jax import lax
from jax.experimental import pallas as pl
from jax.experimental.pallas import tpu as pltpu

D_MODEL = 1024
MEM_LEN = 256
S5_CH_PER_GROUP = 16
S5_GROUPS = 64
S5_STATE = 64
S5_GROUPS_PER_BLOCK = 16
S5_BLOCKS = S5_GROUPS // S5_GROUPS_PER_BLOCK
S5_BLOCK_CH = S5_GROUPS_PER_BLOCK * S5_CH_PER_GROUP
S5_BLOCK_STATE = S5_GROUPS_PER_BLOCK * S5_STATE
D_M2 = 2048
M2_HEADDIM = 64
M2_HEADS = 32
M2_GROUPS = 4
M2_HEADS_PER_GROUP = M2_HEADS // M2_GROUPS
M2_STATE = 128
M2_CONV = 4
M2_CHUNK = 128
D_BC = M2_GROUPS * M2_STATE
D_XBC = D_M2 + 2 * D_BC
XA_HEADS = 4
XA_HEADDIM = 128
D_XA = 512
D_FF = 2816
FFN_CONV = 3
D_GATES = 3 * D_MODEL
DT_PAD = 128
EPS = 1e-6

_O_XBC = 0
_O_GATES = _O_XBC + D_XBC
_O_Z = _O_GATES + D_GATES
_O_Q = _O_Z + D_M2
_O_U = _O_Q + D_XA
_O_DT = _O_U + D_MODEL
_W_IN_COLS = _O_DT + DT_PAD

_INPROJ_STEP = 256
_VMEM_LIMIT = 56 * 1024 * 1024
_F32 = jnp.float32
_BF16 = jnp.bfloat16


def _rms_f32(xf, g):
    inv = lax.rsqrt(jnp.mean(xf * xf, axis=-1, keepdims=True) + EPS)
    return xf * inv * g


def _sigmoid(x):
    return 0.5 * jnp.tanh(0.5 * x) + 0.5


def _dot(a, b):
    return jnp.dot(a, b, preferred_element_type=_F32)


def _resident(shape):
    nd = len(shape)
    return pl.BlockSpec(shape, lambda *_: (0,) * nd, pipeline_mode=pl.Buffered(1))


def _silu(x):
    hx = 0.5 * x
    return hx * (jnp.tanh(hx) + 1.0)


def _inproj_kernel(x_ref, g_ref, w_ref, cw_ref, cb_ref, xc_ref, gates_ref, zs_ref, q_ref, dt_ref, u_ref,
                   ext, halo, *, tm):
    h = _rms_f32(x_ref[...], g_ref[...]).astype(_BF16)
    step = _INPROJ_STEP
    hr = 8
    rb = 32

    def proj(lo, c, width):
        return _dot(h, w_ref[:, lo + c:lo + c + width])

    @pl.when(pl.program_id(1) == 0)
    def _():
        halo[...] = jnp.zeros_like(halo)

    def conv_epilogue(c, slot):
        taps = [cw_ref[k:k + 1, c:c + step] for k in range(M2_CONV)]
        bias = cb_ref[:, c:c + step]
        for r0 in range(0, tm, rb):
            win = ext[slot, r0:r0 + hr + rb, :]
            acc = bias + taps[M2_CONV - 1] * win[hr:]
            for k in range(1, M2_CONV):
                acc = acc + taps[M2_CONV - 1 - k] * pltpu.roll(win, k, 0)[hr:]
            xc_ref[r0:r0 + rb, c:c + step] = _silu(acc).astype(_BF16)

    def plain_epilogue(r, out_ref, c, act, dtype):
        out_ref[:, c:c + r.shape[1]] = (r if act is None else act(r)).astype(dtype)

    convs = [(_O_XBC, c, step, None) for c in range(0, D_XBC, step)]
    plains = []
    for origin, width, out_ref, act, dtype in (
            (_O_GATES, D_GATES, gates_ref, _sigmoid, _BF16), (_O_Z, D_M2, zs_ref, _silu, _BF16),
            (_O_Q, D_XA, q_ref, None, _BF16), (_O_U, D_MODEL, u_ref, None, _BF16),
            (_O_DT, DT_PAD, dt_ref, None, _F32)):
        plains += [(origin, c, min(step, width - c), (out_ref, c, act, dtype)) for c in range(0, width, step)]
    items = []
    for i in range(max(len(convs), (len(plains) + 1) // 2)):
        items += convs[i:i + 1] + plains[2 * i:2 * i + 2]

    pending = None
    for i, (origin, c, width, sink) in enumerate(items):
        r = proj(origin, c, width)
        if sink is None:
            slot = i % 2
            ext[slot, 0:hr, :] = halo[:, c:c + step]
            ext[slot, hr:hr + tm, :] = r
            halo[:, c:c + step] = r[tm - hr:tm]
            epilogue = functools.partial(conv_epilogue, c, slot)
        else:
            epilogue = functools.partial(plain_epilogue, r, *sink)
        if pending is not None:
            pending()
        pending = epilogue
    pending()


def _inproj(x2, norm_g, w_in_r, cw, cb, bsz, seq, tm):
    nt = seq // tm
    tok = bsz * seq
    row = lambda b, t: (b * nt + t, 0)
    return pl.pallas_call(
        functools.partial(_inproj_kernel, tm=tm),
        grid=(bsz, nt),
        in_specs=[
            pl.BlockSpec((tm, D_MODEL), row),
            _resident((1, D_MODEL)),
            _resident((D_MODEL, _W_IN_COLS)),
            _resident((M2_CONV, D_XBC)),
            _resident((1, D_XBC)),
        ],
        out_specs=[
            pl.BlockSpec((tm, D_XBC), row),
            pl.BlockSpec((tm, D_GATES), row),
            pl.BlockSpec((tm, D_M2), row),
            pl.BlockSpec((tm, D_XA), row),
            pl.BlockSpec((tm, DT_PAD), row),
            pl.BlockSpec((tm, D_MODEL), row),
        ],
        out_shape=[
            jax.ShapeDtypeStruct((tok, D_XBC), _BF16),
            jax.ShapeDtypeStruct((tok, D_GATES), _BF16),
            jax.ShapeDtypeStruct((tok, D_M2), _BF16),
            jax.ShapeDtypeStruct((tok, D_XA), _BF16),
            jax.ShapeDtypeStruct((tok, DT_PAD), _F32),
            jax.ShapeDtypeStruct((tok, D_MODEL), _BF16),
        ],
        scratch_shapes=[pltpu.VMEM((2, 8 + tm, _INPROJ_STEP), _F32), pltpu.VMEM((8, D_XBC), _F32)],
        compiler_params=pltpu.CompilerParams(
            dimension_semantics=("arbitrary", "arbitrary"), vmem_limit_bytes=_VMEM_LIMIT),
        name="inproj",
    )(x2, norm_g, w_in_r, cw, cb)


def _gelu_tanh(x):
    c = math.sqrt(2.0 / math.pi)
    return 0.5 * x * (1.0 + jnp.tanh(c * (x + 0.044715 * (x * x * x))))


def _s5_kernel(u_ref, unext_ref, perm_ref, permt_ref, bdb_ref, bdc_ref, are_ref, aim_ref, d_ref, out_ref,
               buf0, buf1, xb0, xb1, ub0, ub1, state, *, bsz, tt, nchunks, lane_chunk):
    ns = S5_BLOCK_STATE
    rows = bsz * tt
    half = rows // 2

    def project_in(u_bt, buf, ub):
        u2 = u_bt.reshape(rows, S5_BLOCK_CH)
        ub[0:half, :] = _dot(perm_ref[0:half, :], u2)
        ub[half:rows, :] = _dot(perm_ref[half:rows, :], u2)
        u_b = ub[...].astype(_BF16)
        buf[:, 0:ns] = _dot(u_b, bdb_ref[:, 0:ns])
        buf[:, ns:2 * ns] = _dot(u_b, bdb_ref[:, ns:2 * ns])

    def scan(buf, xb):
        for c in range(0, ns, lane_chunk):
            ar = jnp.broadcast_to(are_ref[:, c:c + lane_chunk], (bsz, lane_chunk))
            ai = jnp.broadcast_to(aim_ref[:, c:c + lane_chunk], (bsz, lane_chunk))
            xr = state[0, :, c:c + lane_chunk]
            xi = state[1, :, c:c + lane_chunk]
            for t in range(tt):
                r = slice(t * bsz, (t + 1) * bsz)
                nxr = ar * xr - ai * xi + buf[r, c:c + lane_chunk]
                nxi = ar * xi + ai * xr + buf[r, ns + c:ns + c + lane_chunk]
                xb[r, c:c + lane_chunk] = nxr.astype(_BF16)
                xb[r, ns + c:ns + c + lane_chunk] = nxi.astype(_BF16)
                xr, xi = nxr, nxi
            state[0, :, c:c + lane_chunk] = xr
            state[1, :, c:c + lane_chunk] = xi

    def project_out(xb, ub, t0):
        y = _dot(xb[:, 0:ns], bdc_ref[0:ns, :])
        y = y + _dot(xb[:, ns:2 * ns], bdc_ref[ns:2 * ns, :])
        y_tb = (y + d_ref[...] * ub[...]).astype(_BF16)
        hb = bsz // 2
        top = _dot(permt_ref[0:half, :], y_tb).astype(_BF16)
        bot = _dot(permt_ref[half:rows, :], y_tb).astype(_BF16)
        out_ref[0:hb, t0:t0 + tt, :] = top.reshape(hb, tt, S5_BLOCK_CH)
        out_ref[hb:bsz, t0:t0 + tt, :] = bot.reshape(hb, tt, S5_BLOCK_CH)

    bufs, xbs, ubs = (buf0, buf1), (xb0, xb1), (ub0, ub1)

    @pl.when(pl.program_id(1) == 0)
    def _():
        state[...] = jnp.zeros_like(state)
        project_in(u_ref[:, 0:tt, :], buf0, ub0)

    for k in range(nchunks):
        nxt = (k + 1) % 2
        if k + 1 < nchunks:
            project_in(u_ref[:, (k + 1) * tt:(k + 2) * tt, :], bufs[nxt], ubs[nxt])
        else:
            project_in(unext_ref[...], bufs[nxt], ubs[nxt])
        scan(bufs[k % 2], xbs[k % 2])
        project_out(xbs[k % 2], ubs[k % 2], k * tt)


def _s5(u, perm, permt, bdb, bdc, a_re, a_im, d_blk, bsz, seq, tt, nchunks):
    rows = tt * bsz
    nsteps = seq // (nchunks * tt)
    last = seq // tt - 1
    kern = functools.partial(_s5_kernel, bsz=bsz, tt=tt, nchunks=nchunks, lane_chunk=512)
    blk = lambda g, j: (g, 0, 0)
    return pl.pallas_call(
        kern,
        grid=(S5_BLOCKS, nsteps),
        in_specs=[
            pl.BlockSpec((bsz, nchunks * tt, S5_BLOCK_CH), lambda g, j: (0, j, g)),
            pl.BlockSpec((bsz, tt, S5_BLOCK_CH), lambda g, j: (0, jnp.minimum(nchunks * (j + 1), last), g)),
            _resident((rows, rows)),
            _resident((rows, rows)),
            pl.BlockSpec((None, S5_BLOCK_CH, 2 * S5_BLOCK_STATE), blk),
            pl.BlockSpec((None, 2 * S5_BLOCK_STATE, S5_BLOCK_CH), blk),
            pl.BlockSpec((None, 1, S5_BLOCK_STATE), blk),
            pl.BlockSpec((None, 1, S5_BLOCK_STATE), blk),
            pl.BlockSpec((None, 1, S5_BLOCK_CH), blk),
        ],
        out_specs=pl.BlockSpec((bsz, nchunks * tt, S5_BLOCK_CH), lambda g, j: (0, j, g)),
        out_shape=jax.ShapeDtypeStruct((bsz, seq, D_MODEL), _BF16),
        scratch_shapes=[
            pltpu.VMEM((rows, 2 * S5_BLOCK_STATE), _F32),
            pltpu.VMEM((rows, 2 * S5_BLOCK_STATE), _F32),
            pltpu.VMEM((rows, 2 * S5_BLOCK_STATE), _BF16),
            pltpu.VMEM((rows, 2 * S5_BLOCK_STATE), _BF16),
            pltpu.VMEM((rows, S5_BLOCK_CH), _F32),
            pltpu.VMEM((rows, S5_BLOCK_CH), _F32),
            pltpu.VMEM((2, bsz, S5_BLOCK_STATE), _F32),
        ],
        compiler_params=pltpu.CompilerParams(
            dimension_semantics=("arbitrary", "arbitrary"), vmem_limit_bytes=_VMEM_LIMIT),
        name="s5",
    )(u, u, perm, permt, bdb, bdc, a_re, a_im, d_blk)


def _softplus(x):
    return jnp.maximum(x, 0.0) + jnp.log(1.0 + jnp.exp(-jnp.abs(x)))


def _mamba_kernel(xc_ref, zs_ref, dt_ref, dtb_ref, a_ref, dexp_ref, nw_ref, e2_ref, out_ref, hst, ybuf, *, nsub):
    @pl.when(pl.program_id(1) == 0)
    def _():
        hst[...] = jnp.zeros_like(hst)

    for sub in range(nsub):
        _mamba_chunk(sub * M2_CHUNK, xc_ref, zs_ref, dt_ref, dtb_ref, a_ref, dexp_ref, nw_ref, e2_ref, out_ref,
                     hst, ybuf)


def _mamba_chunk(r0, xc_ref, zs_ref, dt_ref, dtb_ref, a_ref, dexp_ref, nw_ref, e2_ref, out_ref, hst, ybuf):
    q = M2_CHUNK
    rs = slice(r0, r0 + q)
    bm = xc_ref[rs, D_M2:D_M2 + D_BC]
    cm = xc_ref[rs, D_M2 + D_BC:D_XBC]

    dt = _softplus(dt_ref[rs, :] + dtb_ref[...])
    da = dt * a_ref[...]
    rowi = lax.broadcasted_iota(jnp.int32, (q, q), 0)
    coli = lax.broadcasted_iota(jnp.int32, (q, q), 1)
    causal = rowi >= coli
    tril = jnp.where(causal, 1.0, 0.0).astype(_F32)
    a_cum = jnp.dot(tril, da, preferred_element_type=_F32, precision=lax.Precision.HIGHEST)
    a_last = a_cum[q - 1:q, :]
    a_cum_t = a_cum.T

    def hilo(v):
        hi = v.astype(_BF16)
        lo = (v - hi.astype(_F32)).astype(_BF16)
        return jnp.concatenate([hi, lo], axis=1)

    pad = 16
    stacked = jnp.concatenate(
        [hilo(dt), hilo(jnp.exp(a_cum)), hilo(jnp.exp(a_last - a_cum)),
         hilo(jnp.broadcast_to(jnp.exp(a_last), (pad, DT_PAD)))], axis=0)
    lane = lax.broadcasted_iota(jnp.int32, (q, 2 * M2_HEADDIM), 1)
    gw = M2_HEADS_PER_GROUP * M2_HEADDIM

    for g in range(M2_GROUPS):
        ex = _dot(stacked, e2_ref[:, g * gw:(g + 1) * gw])
        dt_e = ex[0:q]
        expa_e = ex[q:2 * q]
        dec_e = ex[2 * q:3 * q]
        cd_e = ex[3 * q:3 * q + 1]
        xdt = xc_ref[rs, g * gw:(g + 1) * gw].astype(_F32) * dt_e
        xdt_b = xdt.astype(_BF16)
        xdec_b = (xdt * dec_e).astype(_BF16)
        cg = cm[:, g * M2_STATE:(g + 1) * M2_STATE]
        bg = bm[:, g * M2_STATE:(g + 1) * M2_STATE]
        cb = lax.dot_general(cg, bg, (((1,), (1,)), ((), ())), preferred_element_type=_F32)
        hprev = hst[g]
        yoff = _dot(cg, hprev.astype(_BF16)) * expa_e
        st = lax.dot_general(bg, xdec_b, (((0,), (0,)), ((), ())), preferred_element_type=_F32)
        hst[g] = hprev * cd_e + st
        for pair in range(M2_HEADS_PER_GROUP // 2):
            h0 = g * M2_HEADS_PER_GROUP + 2 * pair
            ms = []
            for hh in (h0, h0 + 1):
                seg = a_cum[:, hh:hh + 1] - a_cum_t[hh:hh + 1, :]
                lm = jnp.exp(jnp.where(causal, seg, -1e30))
                ms.append((cb * lm).astype(_BF16))
            lhs = jnp.concatenate(ms, axis=1)
            xp = xdt_b[:, pair * 2 * M2_HEADDIM:(pair + 1) * 2 * M2_HEADDIM]
            zero = jnp.zeros_like(xp)
            rhs = jnp.concatenate([jnp.where(lane < M2_HEADDIM, xp, zero),
                                   jnp.where(lane >= M2_HEADDIM, xp, zero)], axis=0)
            yd = _dot(lhs, rhs)
            c0 = h0 * M2_HEADDIM
            ybuf[rs, c0:c0 + 2 * M2_HEADDIM] = yd + yoff[:, pair * 2 * M2_HEADDIM:(pair + 1) * 2 * M2_HEADDIM]

    y = ybuf[rs, :] + xc_ref[rs, 0:D_M2].astype(_F32) * dexp_ref[...]
    v = y * zs_ref[rs, :].astype(_F32)
    out_ref[rs, :] = _rms_f32(v, nw_ref[...]).astype(_BF16)


def _mamba(xc, zs, dt, dtb, a_neg, dexp, nw, e2, bsz, seq, nsub):
    q = nsub * M2_CHUNK
    nc = seq // q
    tok = bsz * seq
    row = lambda b, c: (b * nc + c, 0)
    return pl.pallas_call(
        functools.partial(_mamba_kernel, nsub=nsub),
        grid=(bsz, nc),
        in_specs=[
            pl.BlockSpec((q, D_XBC), row),
            pl.BlockSpec((q, D_M2), row),
            pl.BlockSpec((q, DT_PAD), row),
            _resident((1, DT_PAD)),
            _resident((1, DT_PAD)),
            _resident((1, D_M2)),
            _resident((1, D_M2)),
            _resident((2 * DT_PAD, D_M2)),
        ],
        out_specs=pl.BlockSpec((q, D_M2), row),
        out_shape=jax.ShapeDtypeStruct((tok, D_M2), _BF16),
        scratch_shapes=[
            pltpu.VMEM((M2_GROUPS, M2_STATE, M2_HEADS_PER_GROUP * M2_HEADDIM), _F32),
            pltpu.VMEM((q, D_M2), _F32),
        ],
        compiler_params=pltpu.CompilerParams(
            dimension_semantics=("arbitrary", "arbitrary"), vmem_limit_bytes=_VMEM_LIMIT),
        name="mamba",
    )(xc, zs, dt, dtb, a_neg, dexp, nw, e2)


def _kv_kernel(mem_ref, g_ref, w_ref, kv_ref):
    mn = _rms_f32(mem_ref[...], g_ref[...]).astype(_BF16)
    kv_ref[...] = _dot(mn, w_ref[...]).astype(_BF16)


def _kv(mem2, g, w_kv, bsz):
    return pl.pallas_call(
        _kv_kernel,
        grid=(bsz,),
        in_specs=[
            pl.BlockSpec((MEM_LEN, D_MODEL), lambda b: (b, 0)),
            _resident((1, D_MODEL)),
            _resident((D_MODEL, 2 * D_XA)),
        ],
        out_specs=pl.BlockSpec((MEM_LEN, 2 * D_XA), lambda b: (b, 0)),
        out_shape=jax.ShapeDtypeStruct((bsz * MEM_LEN, 2 * D_XA), _BF16),
        compiler_params=pltpu.CompilerParams(
            dimension_semantics=("arbitrary",), vmem_limit_bytes=_VMEM_LIMIT),
        name="kv",
    )(mem2, g, w_kv)


def _attn_kernel(q_ref, kv_ref, out_ref):
    scale = XA_HEADDIM ** -0.5
    for h in range(XA_HEADS):
        lo = h * XA_HEADDIM
        qh = q_ref[:, lo:lo + XA_HEADDIM]
        kh = kv_ref[:, lo:lo + XA_HEADDIM]
        vh = kv_ref[:, D_XA + lo:D_XA + lo + XA_HEADDIM]
        s = lax.dot_general(qh, kh, (((1,), (1,)), ((), ())), preferred_element_type=_F32) * scale
        m = jnp.max(s, axis=-1, keepdims=True)
        p = jnp.exp(s - m).astype(_BF16)
        ov = _dot(p, jnp.concatenate([vh, jnp.ones_like(vh)], axis=1))
        out_ref[:, lo:lo + XA_HEADDIM] = (ov[:, 0:XA_HEADDIM] / ov[:, XA_HEADDIM:]).astype(_BF16)


def _attn(q, kv, bsz, seq, tm):
    nt = seq // tm
    row = lambda b, t: (b * nt + t, 0)
    return pl.pallas_call(
        _attn_kernel,
        grid=(bsz, nt),
        in_specs=[
            pl.BlockSpec((tm, D_XA), row),
            pl.BlockSpec((MEM_LEN, 2 * D_XA), lambda b, t: (b, 0)),
        ],
        out_specs=pl.BlockSpec((tm, D_XA), row),
        out_shape=jax.ShapeDtypeStruct((bsz * seq, D_XA), _BF16),
        compiler_params=pltpu.CompilerParams(
            dimension_semantics=("arbitrary", "arbitrary"), vmem_limit_bytes=_VMEM_LIMIT),
        name="attn",
    )(q, kv)


def _merge_kernel(x_ref, gs5_ref, gates_ref, nb_ref, at_ref, wav_ref, wag_ref, wb_ref, wc_ref, wout_ref,
                  nf_ref, x1_ref, h2_ref):
    g = _gelu_tanh(gs5_ref[...].astype(_F32)).astype(_BF16)
    ya = _dot(g, wav_ref[...]) * _sigmoid(_dot(g, wag_ref[...]))
    m = gates_ref[:, 0:D_MODEL].astype(_F32) * ya
    m = m + gates_ref[:, D_MODEL:2 * D_MODEL].astype(_F32) * _dot(nb_ref[...], wb_ref[...])
    m = m + gates_ref[:, 2 * D_MODEL:].astype(_F32) * _dot(at_ref[...], wc_ref[...])
    x1 = x_ref[...] + _dot(m.astype(_BF16), wout_ref[...])
    x1_ref[...] = x1
    h2_ref[...] = _rms_f32(x1, nf_ref[...]).astype(_BF16)


def _merge(x2, gs5, gates, nb, at, wav, wag, wb, wc, wout, nf, bsz, seq, tm):
    nt = seq // tm
    tok = bsz * seq
    row = lambda b, t: (b * nt + t, 0)
    return pl.pallas_call(
        _merge_kernel,
        grid=(bsz, nt),
        in_specs=[
            pl.BlockSpec((tm, D_MODEL), row),
            pl.BlockSpec((tm, D_MODEL), row),
            pl.BlockSpec((tm, D_GATES), row),
            pl.BlockSpec((tm, D_M2), row),
            pl.BlockSpec((tm, D_XA), row),
            _resident((D_MODEL, D_MODEL)),
            _resident((D_MODEL, D_MODEL)),
            _resident((D_M2, D_MODEL)),
            _resident((D_XA, D_MODEL)),
            _resident((D_MODEL, D_MODEL)),
            _resident((1, D_MODEL)),
        ],
        out_specs=[pl.BlockSpec((tm, D_MODEL), row), pl.BlockSpec((tm, D_MODEL), row)],
        out_shape=[jax.ShapeDtypeStruct((tok, D_MODEL), _F32), jax.ShapeDtypeStruct((tok, D_MODEL), _BF16)],
        compiler_params=pltpu.CompilerParams(
            dimension_semantics=("arbitrary", "arbitrary"), vmem_limit_bytes=_VMEM_LIMIT),
        name="merge",
    )(x2, gs5, gates, nb, at, wav, wag, wb, wc, wout, nf)


def _ffn_kernel(h2_ref, halo_ref, x1_ref, wup_ref, cw_ref, cb_ref, wdown_ref, nfin_ref, out_ref, act_s,
                *, tm, chunk):
    hl = 16
    first = pl.program_id(1) == 0
    halo = jnp.where(first, jnp.zeros((hl, D_MODEL), _BF16), halo_ref[...])
    h2e = jnp.concatenate([halo, h2_ref[...]], axis=0)

    def conv(u, c0):
        v = cb_ref[:, c0:c0 + chunk] + cw_ref[FFN_CONV - 1:FFN_CONV, c0:c0 + chunk] * u[hl:hl + tm]
        for k in range(1, FFN_CONV):
            tap = cw_ref[FFN_CONV - 1 - k:FFN_CONV - k, c0:c0 + chunk]
            v = v + tap * pltpu.roll(u, k, 0)[hl:hl + tm]
        return v

    def up(j):
        ca, cg = j * chunk, D_FF + j * chunk
        return _dot(h2e, wup_ref[:, ca:ca + chunk]), _dot(h2e, wup_ref[:, cg:cg + chunk])

    n = D_FF // chunk
    nxt = up(0)
    for j in range(n):
        ua, ub = nxt
        if j + 1 < n:
            nxt = up(j + 1)
        ca, cg = j * chunk, D_FF + j * chunk
        act_s[:, ca:ca + chunk] = (_silu(conv(ua, ca)) * conv(ub, cg)).astype(_BF16)
    down = _dot(act_s[...], wdown_ref[...])
    out_ref[...] = _rms_f32(x1_ref[...] + down, nfin_ref[...])


def _ffn(h2, x1, wup, cw, cb, wdown, nfin, bsz, seq, tm):
    nt = seq // tm
    tok = bsz * seq
    hl = 16
    chunk = 256
    row = lambda b, t: (b * nt + t, 0)
    halo_row = lambda b, t: (jnp.maximum((b * nt + t) * (tm // hl) - 1, 0), 0)
    kern = functools.partial(_ffn_kernel, tm=tm, chunk=chunk)
    return pl.pallas_call(
        kern,
        grid=(bsz, nt),
        in_specs=[
            pl.BlockSpec((tm, D_MODEL), row),
            pl.BlockSpec((hl, D_MODEL), halo_row),
            pl.BlockSpec((tm, D_MODEL), row),
            _resident((D_MODEL, 2 * D_FF)),
            _resident((FFN_CONV, 2 * D_FF)),
            _resident((1, 2 * D_FF)),
            _resident((D_FF, D_MODEL)),
            _resident((1, D_MODEL)),
        ],
        out_specs=pl.BlockSpec((tm, D_MODEL), row),
        out_shape=jax.ShapeDtypeStruct((tok, D_MODEL), _F32),
        scratch_shapes=[pltpu.VMEM((tm, D_FF), _BF16)],
        compiler_params=pltpu.CompilerParams(
            dimension_semantics=("arbitrary", "arbitrary"), vmem_limit_bytes=_VMEM_LIMIT),
        name="ffn",
    )(h2, h2, x1, wup, cw, cb, wdown, nfin)


def _s5_params(lam_re, lam_im, log_dt, b_re, b_im, c_re, c_im, d):
    dt = jnp.exp(log_dt)[:, None]
    mag = jnp.exp(lam_re * dt)
    ar = mag * jnp.cos(lam_im * dt)
    ai = mag * jnp.sin(lam_im * dt)
    den = lam_re * lam_re + lam_im * lam_im
    fr = ((ar - 1.0) * lam_re + ai * lam_im) / den
    fi = (ai * lam_re - (ar - 1.0) * lam_im) / den
    bbr = fr[..., None] * b_re - fi[..., None] * b_im
    bbi = fr[..., None] * b_im + fi[..., None] * b_re
    eye = jnp.eye(S5_GROUPS_PER_BLOCK, dtype=_F32)
    nb, gl, k, p = S5_BLOCKS, S5_GROUPS_PER_BLOCK, S5_CH_PER_GROUP, S5_STATE

    def bd_in(bb):
        t = bb.transpose(0, 2, 1).reshape(nb, gl, k, 1, p)
        return (t * eye[None, :, None, :, None]).reshape(nb, gl * k, gl * p)

    def bd_out(c):
        t = c.transpose(0, 2, 1).reshape(nb, gl, p, 1, k)
        return (t * eye[None, :, None, :, None]).reshape(nb, gl * p, gl * k)

    bdb = jnp.concatenate([bd_in(bbr), bd_in(bbi)], axis=-1).astype(_BF16)
    bdc = jnp.concatenate([bd_out(c_re), -bd_out(c_im)], axis=1).astype(_BF16)
    a_re = ar.reshape(nb, 1, gl * p)
    a_im = ai.reshape(nb, 1, gl * p)
    d_blk = d.reshape(nb, 1, gl * k)
    return bdb, bdc, a_re, a_im, d_blk


def kernel(x, mem, norm_mix, w_in, s5_lambda_re, s5_lambda_im, s5_log_dt, s5_b_re, s5_b_im, s5_c_re, s5_c_im,
           s5_d, w_a_val, w_a_gate, m2_conv_w, m2_conv_b, m2_dt_bias, m2_a_log, m2_d, m2_norm, w_b, norm_mem,
           w_kv, w_c, w_out, norm_ffn, w_up, ffn_conv_w, ffn_conv_b, w_down, norm_final):
    bsz, seq, _ = x.shape
    depth = w_in.shape[0]
    tm = min(512, seq)
    tt = 16
    s5_chunks = 4
    p1 = D_MODEL
    p2 = p1 + D_M2
    p3 = p2 + D_XBC
    p4 = p3 + M2_HEADS
    p5 = p4 + D_XA
    row1 = lambda v: v.reshape(1, -1).astype(_F32)
    head_expand = (jnp.arange(DT_PAD)[:, None] == (jnp.arange(D_M2)[None, :] // M2_HEADDIM)).astype(_BF16)
    pad_heads = lambda v: jnp.pad(v.astype(_F32), (0, DT_PAD - M2_HEADS)).reshape(1, DT_PAD)

    assert depth == 1, "single-layer problem: the final RMSNorm is fused into the FFN kernel"
    l = 0
    x2 = x.reshape(bsz * seq, D_MODEL)
    mem2 = mem.reshape(bsz * MEM_LEN, D_MODEL)
    w = w_in[l]
    w_in_r = jnp.concatenate(
        [w[:, p2:p3], w[:, p5:], w[:, p1:p2], w[:, p4:p5], w[:, :p1],
         jnp.pad(w[:, p3:p4], ((0, 0), (0, DT_PAD - M2_HEADS)))], axis=1).astype(_BF16)
    xc, gates, zs, q, dt_raw, u = _inproj(x2, row1(norm_mix[l]), w_in_r, m2_conv_w[l].astype(_F32),
                                          row1(m2_conv_b[l]), bsz, seq, tm)

    bdb, bdc, a_re, a_im, d_blk = _s5_params(
        s5_lambda_re[l], s5_lambda_im[l], s5_log_dt[l], s5_b_re[l], s5_b_im[l], s5_c_re[l], s5_c_im[l],
        s5_d[l])
    ridx = jnp.arange(tt * bsz)
    perm = (((ridx % bsz) * tt + ridx // bsz)[:, None] == ridx[None, :]).astype(_BF16)
    gs5 = _s5(u.reshape(bsz, seq, D_MODEL), perm, perm.T, bdb, bdc, a_re, a_im, d_blk, bsz, seq, tt, s5_chunks)

    nb = _mamba(xc, zs, dt_raw, pad_heads(m2_dt_bias[l]), pad_heads(-jnp.exp(m2_a_log[l].astype(_F32))),
                row1(jnp.repeat(m2_d[l], M2_HEADDIM)), row1(m2_norm[l]),
                jnp.concatenate([head_expand, head_expand], axis=0), bsz, seq, min(8, seq // M2_CHUNK))

    kv = _kv(mem2, row1(norm_mem[l]), w_kv[l].astype(_BF16), bsz)
    at = _attn(q, kv, bsz, seq, tm)

    x1, h2 = _merge(x2, gs5.reshape(bsz * seq, D_MODEL), gates, nb, at,
                    w_a_val[l].astype(_BF16), w_a_gate[l].astype(_BF16), w_b[l].astype(_BF16),
                    w_c[l].astype(_BF16), w_out[l].astype(_BF16), row1(norm_ffn[l]), bsz, seq, tm)

    out = _ffn(h2, x1, w_up[l].astype(_BF16), ffn_conv_w[l].astype(_F32), row1(ffn_conv_b[l]),
               w_down[l].astype(_BF16), row1(norm_final), bsz, seq, tm)
    return out.reshape(bsz, seq, D_MODEL)
```

```python
import functools
import math

import jax
import jax.numpy as jnp
from jax import lax
from jax.experimental import pallas as pl
from jax.experimental.pallas import tpu as pltpu

D_MODEL = 1024
MEM_LEN = 256
S5_CH_PER_GROUP = 16
S5_GROUPS = 64
S5_STATE = 64
S5_GROUPS_PER_BLOCK = 16
S5_BLOCKS = S5_GROUPS // S5_GROUPS_PER_BLOCK
S5_BLOCK_CH = S5_GROUPS_PER_BLOCK * S5_CH_PER_GROUP
S5_BLOCK_STATE = S5_GROUPS_PER_BLOCK * S5_STATE
D_M2 = 2048
M2_HEADDIM = 64
M2_HEADS = 32
M2_GROUPS = 4
M2_HEADS_PER_GROUP = M2_HEADS // M2_GROUPS
M2_STATE = 128
M2_CONV = 4
M2_CHUNK = 128
D_BC = M2_GROUPS * M2_STATE
D_XBC = D_M2 + 2 * D_BC
XA_HEADS = 4
XA_HEADDIM = 128
D_XA = 512
D_FF = 2816
FFN_CONV = 3
D_GATES = 3 * D_MODEL
DT_PAD = 128
EPS = 1e-6

_O_XBC = 0
_O_GATES = _O_XBC + D_XBC
_O_Z = _O_GATES + D_GATES
_O_Q = _O_Z + D_M2
_O_U = _O_Q + D_XA
_O_DT = _O_U + D_MODEL
_W_IN_COLS = _O_DT + DT_PAD

_INPROJ_STEP = 256
_VMEM_LIMIT = 56 * 1024 * 1024
_F32 = jnp.float32
_BF16 = jnp.bfloat16


def _rms_f32(xf, g):
    inv = lax.rsqrt(jnp.mean(xf * xf, axis=-1, keepdims=True) + EPS)
    return xf * inv * g


def _sigmoid(x):
    return 0.5 * jnp.tanh(0.5 * x) + 0.5


def _dot(a, b):
    return jnp.dot(a, b, preferred_element_type=_F32)


def _resident(shape):
    nd = len(shape)
    return pl.BlockSpec(shape, lambda *_: (0,) * nd, pipeline_mode=pl.Buffered(1))


def _silu(x):
    hx = 0.5 * x
    return hx * (jnp.tanh(hx) + 1.0)


def _inproj_kernel(x_ref, g_ref, w_ref, cw_ref, cb_ref, xc_ref, gates_ref, z_ref, q_ref, dt_ref, u_ref,
                   ext, halo, *, tm):
    h = _rms_f32(x_ref[...], g_ref[...]).astype(_BF16)
    step = _INPROJ_STEP
    hr = 8
    rb = 32

    def proj(lo, c, width):
        return _dot(h, w_ref[:, lo + c:lo + c + width])

    @pl.when(pl.program_id(1) == 0)
    def _():
        halo[...] = jnp.zeros_like(halo)

    def conv_epilogue(c, slot):
        taps = [cw_ref[k:k + 1, c:c + step] for k in range(M2_CONV)]
        bias = cb_ref[:, c:c + step]
        for r0 in range(0, tm, rb):
            win = ext[slot, r0:r0 + hr + rb, :]
            acc = bias + taps[M2_CONV - 1] * win[hr:]
            for k in range(1, M2_CONV):
                acc = acc + taps[M2_CONV - 1 - k] * pltpu.roll(win, k, 0)[hr:]
            xc_ref[r0:r0 + rb, c:c + step] = _silu(acc).astype(_BF16)

    def plain_epilogue(r, out_ref, c, act, dtype):
        out_ref[:, c:c + r.shape[1]] = (r if act is None else act(r)).astype(dtype)

    convs = [(_O_XBC, c, step, None) for c in range(0, D_XBC, step)]
    plains = []
    for origin, width, out_ref, act, dtype in (
            (_O_GATES, D_GATES, gates_ref, None, _BF16), (_O_Z, D_M2, z_ref, None, _BF16),
            (_O_Q, D_XA, q_ref, None, _BF16), (_O_U, D_MODEL, u_ref, None, _BF16),
            (_O_DT, DT_PAD, dt_ref, None, _F32)):
        plains += [(origin, c, min(step, width - c), (out_ref, c, act, dtype)) for c in range(0, width, step)]
    items = []
    for i in range(max(len(convs), (len(plains) + 1) // 2)):
        items += convs[i:i + 1] + plains[2 * i:2 * i + 2]

    pending = None
    for i, (origin, c, width, sink) in enumerate(items):
        r = proj(origin, c, width)
        if sink is None:
            slot = i % 2
            ext[slot, 0:hr, :] = halo[:, c:c + step]
            ext[slot, hr:hr + tm, :] = r
            halo[:, c:c + step] = r[tm - hr:tm]
            epilogue = functools.partial(conv_epilogue, c, slot)
        else:
            epilogue = functools.partial(plain_epilogue, r, *sink)
        if pending is not None:
            pending()
        pending = epilogue
    pending()


def _inproj(x2, norm_g, w_in_r, cw, cb, bsz, seq, tm):
    nt = seq // tm
    tok = bsz * seq
    row = lambda b, t: (b * nt + t, 0)
    return pl.pallas_call(
        functools.partial(_inproj_kernel, tm=tm),
        grid=(bsz, nt),
        in_specs=[
            pl.BlockSpec((tm, D_MODEL), row),
            _resident((1, D_MODEL)),
            _resident((D_MODEL, _W_IN_COLS)),
            _resident((M2_CONV, D_XBC)),
            _resident((1, D_XBC)),
        ],
        out_specs=[
            pl.BlockSpec((tm, D_XBC), row),
            pl.BlockSpec((tm, D_GATES), row),
            pl.BlockSpec((tm, D_M2), row),
            pl.BlockSpec((tm, D_XA), row),
            pl.BlockSpec((tm, DT_PAD), row),
            pl.BlockSpec((tm, D_MODEL), row),
        ],
        out_shape=[
            jax.ShapeDtypeStruct((tok, D_XBC), _BF16),
            jax.ShapeDtypeStruct((tok, D_GATES), _BF16),
            jax.ShapeDtypeStruct((tok, D_M2), _BF16),
            jax.ShapeDtypeStruct((tok, D_XA), _BF16),
            jax.ShapeDtypeStruct((tok, DT_PAD), _F32),
            jax.ShapeDtypeStruct((tok, D_MODEL), _BF16),
        ],
        scratch_shapes=[pltpu.VMEM((2, 8 + tm, _INPROJ_STEP), _F32), pltpu.VMEM((8, D_XBC), _F32)],
        compiler_params=pltpu.CompilerParams(
            dimension_semantics=("arbitrary", "arbitrary"), vmem_limit_bytes=_VMEM_LIMIT),
        name="inproj",
    )(x2, norm_g, w_in_r, cw, cb)


def _gelu_tanh(x):
    c = math.sqrt(2.0 / math.pi)
    return 0.5 * x * (1.0 + jnp.tanh(c * (x + 0.044715 * (x * x * x))))


def _s5_kernel(u_ref, unext_ref, perm_ref, permt_ref, bdb_ref, bdc_ref, are_ref, aim_ref, d_ref, out_ref,
               buf0, buf1, xb0, xb1, ub0, ub1, state, *, bsz, tt, nchunks, lane_chunk):
    ns = S5_BLOCK_STATE
    rows = bsz * tt
    half = rows // 2

    def project_in(u_bt, buf, ub):
        u2 = u_bt.reshape(rows, S5_BLOCK_CH)
        ub[0:half, :] = _dot(perm_ref[0:half, :], u2)
        ub[half:rows, :] = _dot(perm_ref[half:rows, :], u2)
        u_b = ub[...].astype(_BF16)
        buf[:, 0:ns] = _dot(u_b, bdb_ref[:, 0:ns])
        buf[:, ns:2 * ns] = _dot(u_b, bdb_ref[:, ns:2 * ns])

    def scan(buf, xb):
        for c in range(0, ns, lane_chunk):
            ar = jnp.broadcast_to(are_ref[:, c:c + lane_chunk], (bsz, lane_chunk))
            ai = jnp.broadcast_to(aim_ref[:, c:c + lane_chunk], (bsz, lane_chunk))
            xr = state[0, :, c:c + lane_chunk]
            xi = state[1, :, c:c + lane_chunk]
            for t in range(tt):
                r = slice(t * bsz, (t + 1) * bsz)
                nxr = ar * xr - ai * xi + buf[r, c:c + lane_chunk]
                nxi = ar * xi + ai * xr + buf[r, ns + c:ns + c + lane_chunk]
                xb[r, c:c + lane_chunk] = nxr.astype(_BF16)
                xb[r, ns + c:ns + c + lane_chunk] = nxi.astype(_BF16)
                xr, xi = nxr, nxi
            state[0, :, c:c + lane_chunk] = xr
            state[1, :, c:c + lane_chunk] = xi

    def project_out(xb, ub, t0):
        y = _dot(xb[:, 0:ns], bdc_ref[0:ns, :])
        y = y + _dot(xb[:, ns:2 * ns], bdc_ref[ns:2 * ns, :])
        y_tb = (y + d_ref[...] * ub[...]).astype(_BF16)
        hb = bsz // 2
        top = _dot(permt_ref[0:half, :], y_tb).astype(_BF16)
        bot = _dot(permt_ref[half:rows, :], y_tb).astype(_BF16)
        out_ref[0:hb, t0:t0 + tt, :] = top.reshape(hb, tt, S5_BLOCK_CH)
        out_ref[hb:bsz, t0:t0 + tt, :] = bot.reshape(hb, tt, S5_BLOCK_CH)

    bufs, xbs, ubs = (buf0, buf1), (xb0, xb1), (ub0, ub1)

    @pl.when(pl.program_id(1) == 0)
    def _():
        state[...] = jnp.zeros_like(state)
        project_in(u_ref[:, 0:tt, :], buf0, ub0)

    for k in range(nchunks):
        nxt = (k + 1) % 2
        if k + 1 < nchunks:
            project_in(u_ref[:, (k + 1) * tt:(k + 2) * tt, :], bufs[nxt], ubs[nxt])
        else:
            project_in(unext_ref[...], bufs[nxt], ubs[nxt])
        scan(bufs[k % 2], xbs[k % 2])
        project_out(xbs[k % 2], ubs[k % 2], k * tt)


def _s5(u, perm, permt, bdb, bdc, a_re, a_im, d_blk, bsz, seq, tt, nchunks):
    rows = tt * bsz
    nsteps = seq // (nchunks * tt)
    last = seq // tt - 1
    kern = functools.partial(_s5_kernel, bsz=bsz, tt=tt, nchunks=nchunks, lane_chunk=512)
    blk = lambda g, j: (g, 0, 0)
    return pl.pallas_call(
        kern,
        grid=(S5_BLOCKS, nsteps),
        in_specs=[
            pl.BlockSpec((bsz, nchunks * tt, S5_BLOCK_CH), lambda g, j: (0, j, g)),
            pl.BlockSpec((bsz, tt, S5_BLOCK_CH), lambda g, j: (0, jnp.minimum(nchunks * (j + 1), last), g)),
            _resident((rows, rows)),
            _resident((rows, rows)),
            pl.BlockSpec((None, S5_BLOCK_CH, 2 * S5_BLOCK_STATE), blk),
            pl.BlockSpec((None, 2 * S5_BLOCK_STATE, S5_BLOCK_CH), blk),
            pl.BlockSpec((None, 1, S5_BLOCK_STATE), blk),
            pl.BlockSpec((None, 1, S5_BLOCK_STATE), blk),
            pl.BlockSpec((None, 1, S5_BLOCK_CH), blk),
        ],
        out_specs=pl.BlockSpec((bsz, nchunks * tt, S5_BLOCK_CH), lambda g, j: (0, j, g)),
        out_shape=jax.ShapeDtypeStruct((bsz, seq, D_MODEL), _BF16),
        scratch_shapes=[
            pltpu.VMEM((rows, 2 * S5_BLOCK_STATE), _F32),
            pltpu.VMEM((rows, 2 * S5_BLOCK_STATE), _F32),
            pltpu.VMEM((rows, 2 * S5_BLOCK_STATE), _BF16),
            pltpu.VMEM((rows, 2 * S5_BLOCK_STATE), _BF16),
            pltpu.VMEM((rows, S5_BLOCK_CH), _F32),
            pltpu.VMEM((rows, S5_BLOCK_CH), _F32),
            pltpu.VMEM((2, bsz, S5_BLOCK_STATE), _F32),
        ],
        compiler_params=pltpu.CompilerParams(
            dimension_semantics=("arbitrary", "arbitrary"), vmem_limit_bytes=_VMEM_LIMIT),
        name="s5",
    )(u, u, perm, permt, bdb, bdc, a_re, a_im, d_blk)


def _softplus(x):
    return jnp.maximum(x, 0.0) + jnp.log(1.0 + jnp.exp(-jnp.abs(x)))


def _mamba_kernel(xc_ref, z_ref, dt_ref, dtb_ref, a_ref, dexp_ref, nw_ref, e2_ref, out_ref, hst, ybuf, *, nsub):
    @pl.when(pl.program_id(1) == 0)
    def _():
        hst[...] = jnp.zeros_like(hst)

    for sub in range(nsub):
        _mamba_chunk(sub * M2_CHUNK, xc_ref, z_ref, dt_ref, dtb_ref, a_ref, dexp_ref, nw_ref, e2_ref, out_ref,
                     hst, ybuf)


def _mamba_chunk(r0, xc_ref, z_ref, dt_ref, dtb_ref, a_ref, dexp_ref, nw_ref, e2_ref, out_ref, hst, ybuf):
    q = M2_CHUNK
    rs = slice(r0, r0 + q)
    bm = xc_ref[rs, D_M2:D_M2 + D_BC]
    cm = xc_ref[rs, D_M2 + D_BC:D_XBC]

    dt = _softplus(dt_ref[rs, :] + dtb_ref[...])
    da = dt * a_ref[...]
    rowi = lax.broadcasted_iota(jnp.int32, (q, q), 0)
    coli = lax.broadcasted_iota(jnp.int32, (q, q), 1)
    causal = rowi >= coli
    tril = jnp.where(causal, 1.0, 0.0).astype(_F32)
    a_cum = jnp.dot(tril, da, preferred_element_type=_F32, precision=lax.Precision.HIGHEST)
    a_last = a_cum[q - 1:q, :]
    a_cum_t = a_cum.T

    def hilo(v):
        hi = v.astype(_BF16)
        lo = (v - hi.astype(_F32)).astype(_BF16)
        return jnp.concatenate([hi, lo], axis=1)

    pad = 16
    stacked = jnp.concatenate(
        [hilo(dt), hilo(jnp.exp(a_cum)), hilo(jnp.exp(a_last - a_cum)),
         hilo(jnp.broadcast_to(jnp.exp(a_last), (pad, DT_PAD)))], axis=0)
    lane = lax.broadcasted_iota(jnp.int32, (q, 2 * M2_HEADDIM), 1)
    gw = M2_HEADS_PER_GROUP * M2_HEADDIM

    for g in range(M2_GROUPS):
        ex = _dot(stacked, e2_ref[:, g * gw:(g + 1) * gw])
        dt_e = ex[0:q]
        expa_e = ex[q:2 * q]
        dec_e = ex[2 * q:3 * q]
        cd_e = ex[3 * q:3 * q + 1]
        xdt = xc_ref[rs, g * gw:(g + 1) * gw].astype(_F32) * dt_e
        xdt_b = xdt.astype(_BF16)
        xdec_b = (xdt * dec_e).astype(_BF16)
        cg = cm[:, g * M2_STATE:(g + 1) * M2_STATE]
        bg = bm[:, g * M2_STATE:(g + 1) * M2_STATE]
        cb = lax.dot_general(cg, bg, (((1,), (1,)), ((), ())), preferred_element_type=_F32)
        hprev = hst[g]
        yoff = _dot(cg, hprev.astype(_BF16)) * expa_e
        st = lax.dot_general(bg, xdec_b, (((0,), (0,)), ((), ())), preferred_element_type=_F32)
        hst[g] = hprev * cd_e + st
        for pair in range(M2_HEADS_PER_GROUP // 2):
            h0 = g * M2_HEADS_PER_GROUP + 2 * pair
            ms = []
            for hh in (h0, h0 + 1):
                seg = a_cum[:, hh:hh + 1] - a_cum_t[hh:hh + 1, :]
                lm = jnp.exp(jnp.where(causal, seg, -1e30))
                ms.append((cb * lm).astype(_BF16))
            lhs = jnp.concatenate(ms, axis=1)
            xp = xdt_b[:, pair * 2 * M2_HEADDIM:(pair + 1) * 2 * M2_HEADDIM]
            zero = jnp.zeros_like(xp)
            rhs = jnp.concatenate([jnp.where(lane < M2_HEADDIM, xp, zero),
                                   jnp.where(lane >= M2_HEADDIM, xp, zero)], axis=0)
            yd = _dot(lhs, rhs)
            c0 = h0 * M2_HEADDIM
            ybuf[rs, c0:c0 + 2 * M2_HEADDIM] = yd + yoff[:, pair * 2 * M2_HEADDIM:(pair + 1) * 2 * M2_HEADDIM]

    y = ybuf[rs, :] + xc_ref[rs, 0:D_M2].astype(_F32) * dexp_ref[...]
    v = y * _silu(z_ref[rs, :].astype(_F32))
    out_ref[rs, :] = _rms_f32(v, nw_ref[...]).astype(_BF16)


def _mamba(xc, z, dt, dtb, a_neg, dexp, nw, e2, bsz, seq, nsub):
    q = nsub * M2_CHUNK
    nc = seq // q
    tok = bsz * seq
    row = lambda b, c: (b * nc + c, 0)
    return pl.pallas_call(
        functools.partial(_mamba_kernel, nsub=nsub),
        grid=(bsz, nc),
        in_specs=[
            pl.BlockSpec((q, D_XBC), row),
            pl.BlockSpec((q, D_M2), row),
            pl.BlockSpec((q, DT_PAD), row),
            _resident((1, DT_PAD)),
            _resident((1, DT_PAD)),
            _resident((1, D_M2)),
            _resident((1, D_M2)),
            _resident((2 * DT_PAD, D_M2)),
        ],
        out_specs=pl.BlockSpec((q, D_M2), row),
        out_shape=jax.ShapeDtypeStruct((tok, D_M2), _BF16),
        scratch_shapes=[
            pltpu.VMEM((M2_GROUPS, M2_STATE, M2_HEADS_PER_GROUP * M2_HEADDIM), _F32),
            pltpu.VMEM((q, D_M2), _F32),
        ],
        compiler_params=pltpu.CompilerParams(
            dimension_semantics=("arbitrary", "arbitrary"), vmem_limit_bytes=_VMEM_LIMIT),
        name="mamba",
    )(xc, z, dt, dtb, a_neg, dexp, nw, e2)


def _kv_kernel(mem_ref, g_ref, w_ref, kv_ref):
    mn = _rms_f32(mem_ref[...], g_ref[...]).astype(_BF16)
    kv_ref[...] = _dot(mn, w_ref[...]).astype(_BF16)


def _kv(mem2, g, w_kv, bsz):
    return pl.pallas_call(
        _kv_kernel,
        grid=(bsz,),
        in_specs=[
            pl.BlockSpec((MEM_LEN, D_MODEL), lambda b: (b, 0)),
            _resident((1, D_MODEL)),
            _resident((D_MODEL, 2 * D_XA)),
        ],
        out_specs=pl.BlockSpec((MEM_LEN, 2 * D_XA), lambda b: (b, 0)),
        out_shape=jax.ShapeDtypeStruct((bsz * MEM_LEN, 2 * D_XA), _BF16),
        compiler_params=pltpu.CompilerParams(
            dimension_semantics=("arbitrary",), vmem_limit_bytes=_VMEM_LIMIT),
        name="kv",
    )(mem2, g, w_kv)


def _attn_kernel(q_ref, kv_ref, out_ref):
    scale = XA_HEADDIM ** -0.5
    for h in range(XA_HEADS):
        lo = h * XA_HEADDIM
        qh = q_ref[:, lo:lo + XA_HEADDIM]
        kh = kv_ref[:, lo:lo + XA_HEADDIM]
        vh = kv_ref[:, D_XA + lo:D_XA + lo + XA_HEADDIM]
        s = lax.dot_general(qh, kh, (((1,), (1,)), ((), ())), preferred_element_type=_F32) * scale
        m = jnp.max(s, axis=-1, keepdims=True)
        p = jnp.exp(s - m).astype(_BF16)
        ov = _dot(p, jnp.concatenate([vh, jnp.ones_like(vh)], axis=1))
        out_ref[:, lo:lo + XA_HEADDIM] = (ov[:, 0:XA_HEADDIM] / ov[:, XA_HEADDIM:]).astype(_BF16)


def _attn(q, kv, bsz, seq, tm):
    nt = seq // tm
    row = lambda b, t: (b * nt + t, 0)
    return pl.pallas_call(
        _attn_kernel,
        grid=(bsz, nt),
        in_specs=[
            pl.BlockSpec((tm, D_XA), row),
            pl.BlockSpec((MEM_LEN, 2 * D_XA), lambda b, t: (b, 0)),
        ],
        out_specs=pl.BlockSpec((tm, D_XA), row),
        out_shape=jax.ShapeDtypeStruct((bsz * seq, D_XA), _BF16),
        compiler_params=pltpu.CompilerParams(
            dimension_semantics=("arbitrary", "arbitrary"), vmem_limit_bytes=_VMEM_LIMIT),
        name="attn",
    )(q, kv)


def _merge_kernel(x_ref, gs5_ref, gates_ref, nb_ref, at_ref, wav_ref, wag_ref, wb_ref, wc_ref, wout_ref,
                  nf_ref, x1_ref, h2_ref):
    g = _gelu_tanh(gs5_ref[...].astype(_F32)).astype(_BF16)
    ya = _dot(g, wav_ref[...]) * _sigmoid(_dot(g, wag_ref[...]))
    gate = lambda i: _sigmoid(gates_ref[:, i * D_MODEL:(i + 1) * D_MODEL].astype(_F32))
    m = gate(0) * ya
    m = m + gate(1) * _dot(nb_ref[...], wb_ref[...])
    m = m + gate(2) * _dot(at_ref[...], wc_ref[...])
    x1 = x_ref[...] + _dot(m.astype(_BF16), wout_ref[...])
    x1_ref[...] = x1
    h2_ref[...] = _rms_f32(x1, nf_ref[...]).astype(_BF16)


def _merge(x2, gs5, gates, nb, at, wav, wag, wb, wc, wout, nf, bsz, seq, tm):
    nt = seq // tm
    tok = bsz * seq
    row = lambda b, t: (b * nt + t, 0)
    return pl.pallas_call(
        _merge_kernel,
        grid=(bsz, nt),
        in_specs=[
            pl.BlockSpec((tm, D_MODEL), row),
            pl.BlockSpec((tm, D_MODEL), row),
            pl.BlockSpec((tm, D_GATES), row),
            pl.BlockSpec((tm, D_M2), row),
            pl.BlockSpec((tm, D_XA), row),
            _resident((D_MODEL, D_MODEL)),
            _resident((D_MODEL, D_MODEL)),
            _resident((D_M2, D_MODEL)),
            _resident((D_XA, D_MODEL)),
            _resident((D_MODEL, D_MODEL)),
            _resident((1, D_MODEL)),
        ],
        out_specs=[pl.BlockSpec((tm, D_MODEL), row), pl.BlockSpec((tm, D_MODEL), row)],
        out_shape=[jax.ShapeDtypeStruct((tok, D_MODEL), _F32), jax.ShapeDtypeStruct((tok, D_MODEL), _BF16)],
        compiler_params=pltpu.CompilerParams(
            dimension_semantics=("arbitrary", "arbitrary"), vmem_limit_bytes=_VMEM_LIMIT),
        name="merge",
    )(x2, gs5, gates, nb, at, wav, wag, wb, wc, wout, nf)


def _ffn_kernel(h2_ref, halo_ref, x1_ref, wup_ref, cw_ref, cb_ref, wdown_ref, nfin_ref, out_ref, act_s,
                *, tm, chunk):
    hl = 16
    first = pl.program_id(1) == 0
    halo = jnp.where(first, jnp.zeros((hl, D_MODEL), _BF16), halo_ref[...])
    h2e = jnp.concatenate([halo, h2_ref[...]], axis=0)

    def conv(u, c0):
        v = cb_ref[:, c0:c0 + chunk] + cw_ref[FFN_CONV - 1:FFN_CONV, c0:c0 + chunk] * u[hl:hl + tm]
        for k in range(1, FFN_CONV):
            tap = cw_ref[FFN_CONV - 1 - k:FFN_CONV - k, c0:c0 + chunk]
            v = v + tap * pltpu.roll(u, k, 0)[hl:hl + tm]
        return v

    def up(j):
        ca, cg = j * chunk, D_FF + j * chunk
        return _dot(h2e, wup_ref[:, ca:ca + chunk]), _dot(h2e, wup_ref[:, cg:cg + chunk])

    n = D_FF // chunk
    nxt = up(0)
    for j in range(n):
        ua, ub = nxt
        if j + 1 < n:
            nxt = up(j + 1)
        ca, cg = j * chunk, D_FF + j * chunk
        act_s[:, ca:ca + chunk] = (_silu(conv(ua, ca)) * conv(ub, cg)).astype(_BF16)
    down = _dot(act_s[...], wdown_ref[...])
    out_ref[...] = _rms_f32(x1_ref[...] + down, nfin_ref[...])


def _ffn(h2, x1, wup, cw, cb, wdown, nfin, bsz, seq, tm):
    nt = seq // tm
    tok = bsz * seq
    hl = 16
    chunk = 256
    row = lambda b, t: (b * nt + t, 0)
    halo_row = lambda b, t: (jnp.maximum((b * nt + t) * (tm // hl) - 1, 0), 0)
    kern = functools.partial(_ffn_kernel, tm=tm, chunk=chunk)
    return pl.pallas_call(
        kern,
        grid=(bsz, nt),
        in_specs=[
            pl.BlockSpec((tm, D_MODEL), row),
            pl.BlockSpec((hl, D_MODEL), halo_row),
            pl.BlockSpec((tm, D_MODEL), row),
            _resident((D_MODEL, 2 * D_FF)),
            _resident((FFN_CONV, 2 * D_FF)),
            _resident((1, 2 * D_FF)),
            _resident((D_FF, D_MODEL)),
            _resident((1, D_MODEL)),
        ],
        out_specs=pl.BlockSpec((tm, D_MODEL), row),
        out_shape=jax.ShapeDtypeStruct((tok, D_MODEL), _F32),
        scratch_shapes=[pltpu.VMEM((tm, D_FF), _BF16)],
        compiler_params=pltpu.CompilerParams(
            dimension_semantics=("arbitrary", "arbitrary"), vmem_limit_bytes=_VMEM_LIMIT),
        name="ffn",
    )(h2, h2, x1, wup, cw, cb, wdown, nfin)


def _s5_params(lam_re, lam_im, log_dt, b_re, b_im, c_re, c_im, d):
    dt = jnp.exp(log_dt)[:, None]
    mag = jnp.exp(lam_re * dt)
    ar = mag * jnp.cos(lam_im * dt)
    ai = mag * jnp.sin(lam_im * dt)
    den = lam_re * lam_re + lam_im * lam_im
    fr = ((ar - 1.0) * lam_re + ai * lam_im) / den
    fi = (ai * lam_re - (ar - 1.0) * lam_im) / den
    bbr = fr[..., None] * b_re - fi[..., None] * b_im
    bbi = fr[..., None] * b_im + fi[..., None] * b_re
    eye = jnp.eye(S5_GROUPS_PER_BLOCK, dtype=_F32)
    nb, gl, k, p = S5_BLOCKS, S5_GROUPS_PER_BLOCK, S5_CH_PER_GROUP, S5_STATE

    def bd_in(bb):
        t = bb.transpose(0, 2, 1).reshape(nb, gl, k, 1, p)
        return (t * eye[None, :, None, :, None]).reshape(nb, gl * k, gl * p)

    def bd_out(c):
        t = c.transpose(0, 2, 1).reshape(nb, gl, p, 1, k)
        return (t * eye[None, :, None, :, None]).reshape(nb, gl * p, gl * k)

    bdb = jnp.concatenate([bd_in(bbr), bd_in(bbi)], axis=-1).astype(_BF16)
    bdc = jnp.concatenate([bd_out(c_re), -bd_out(c_im)], axis=1).astype(_BF16)
    a_re = ar.reshape(nb, 1, gl * p)
    a_im = ai.reshape(nb, 1, gl * p)
    d_blk = d.reshape(nb, 1, gl * k)
    return bdb, bdc, a_re, a_im, d_blk


def kernel(x, mem, norm_mix, w_in, s5_lambda_re, s5_lambda_im, s5_log_dt, s5_b_re, s5_b_im, s5_c_re, s5_c_im,
           s5_d, w_a_val, w_a_gate, m2_conv_w, m2_conv_b, m2_dt_bias, m2_a_log, m2_d, m2_norm, w_b, norm_mem,
           w_kv, w_c, w_out, norm_ffn, w_up, ffn_conv_w, ffn_conv_b, w_down, norm_final):
    bsz, seq, _ = x.shape
    depth = w_in.shape[0]
    tm = min(512, seq)
    tt = 16
    s5_chunks = 8
    p1 = D_MODEL
    p2 = p1 + D_M2
    p3 = p2 + D_XBC
    p4 = p3 + M2_HEADS
    p5 = p4 + D_XA
    row1 = lambda v: v.reshape(1, -1).astype(_F32)
    head_expand = (jnp.arange(DT_PAD)[:, None] == (jnp.arange(D_M2)[None, :] // M2_HEADDIM)).astype(_BF16)
    pad_heads = lambda v: jnp.pad(v.astype(_F32), (0, DT_PAD - M2_HEADS)).reshape(1, DT_PAD)

    assert depth == 1, "single-layer problem: the final RMSNorm is fused into the FFN kernel"
    l = 0
    x2 = x.reshape(bsz * seq, D_MODEL)
    mem2 = mem.reshape(bsz * MEM_LEN, D_MODEL)
    w = w_in[l].astype(_BF16)
    w_in_r = jnp.concatenate(
        [w[:, p2:p3], w[:, p5:], w[:, p1:p2], w[:, p4:p5], w[:, :p1],
         jnp.pad(w[:, p3:p4], ((0, 0), (0, DT_PAD - M2_HEADS)))], axis=1)
    xc, gates, z, q, dt_raw, u = _inproj(x2, row1(norm_mix[l]), w_in_r, m2_conv_w[l].astype(_F32),
                                          row1(m2_conv_b[l]), bsz, seq, tm)

    bdb, bdc, a_re, a_im, d_blk = _s5_params(
        s5_lambda_re[l], s5_lambda_im[l], s5_log_dt[l], s5_b_re[l], s5_b_im[l], s5_c_re[l], s5_c_im[l],
        s5_d[l])
    ridx = jnp.arange(tt * bsz)
    perm = (((ridx % bsz) * tt + ridx // bsz)[:, None] == ridx[None, :]).astype(_BF16)
    gs5 = _s5(u.reshape(bsz, seq, D_MODEL), perm, perm.T, bdb, bdc, a_re, a_im, d_blk, bsz, seq, tt, s5_chunks)

    nb = _mamba(xc, z, dt_raw, pad_heads(m2_dt_bias[l]), pad_heads(-jnp.exp(m2_a_log[l].astype(_F32))),
                row1(jnp.repeat(m2_d[l], M2_HEADDIM)), row1(m2_norm[l]),
                jnp.concatenate([head_expand, head_expand], axis=0), bsz, seq, min(8, seq // M2_CHUNK))

    kv = _kv(mem2, row1(norm_mem[l]), w_kv[l].astype(_BF16), bsz)
    at = _attn(q, kv, bsz, seq, tm)

    x1, h2 = _merge(x2, gs5.reshape(bsz * seq, D_MODEL), gates, nb, at,
                    w_a_val[l].astype(_BF16), w_a_gate[l].astype(_BF16), w_b[l].astype(_BF16),
                    w_c[l].astype(_BF16), w_out[l].astype(_BF16), row1(norm_ffn[l]), bsz, seq, tm)

    out = _ffn(h2, x1, w_up[l].astype(_BF16), ffn_conv_w[l].astype(_F32), row1(ffn_conv_b[l]),
               w_down[l].astype(_BF16), row1(norm_final), bsz, seq, tm)
    return out.reshape(bsz, seq, D_MODEL)
```

```python
import functools
import math

import jax
import jax.numpy as jnp
from jax import lax
from jax.experimental import pallas as pl
from jax.experimental.pallas import tpu as pltpu

D_MODEL = 1024
MEM_LEN = 256
S5_CH_PER_GROUP = 16
S5_GROUPS = 64
S5_STATE = 64
S5_GROUPS_PER_BLOCK = 16
S5_BLOCKS = S5_GROUPS // S5_GROUPS_PER_BLOCK
S5_BLOCK_CH = S5_GROUPS_PER_BLOCK * S5_CH_PER_GROUP
S5_BLOCK_STATE = S5_GROUPS_PER_BLOCK * S5_STATE
D_M2 = 2048
M2_HEADDIM = 64
M2_HEADS = 32
M2_GROUPS = 4
M2_HEADS_PER_GROUP = M2_HEADS // M2_GROUPS
M2_STATE = 128
M2_CONV = 4
M2_CHUNK = 128
D_BC = M2_GROUPS * M2_STATE
D_XBC = D_M2 + 2 * D_BC
XA_HEADS = 4
XA_HEADDIM = 128
D_XA = 512
D_FF = 2816
FFN_CONV = 3
D_GATES = 3 * D_MODEL
DT_PAD = 128
EPS = 1e-6

_O_XBC = 0
_O_GATES = _O_XBC + D_XBC
_O_Z = _O_GATES + D_GATES
_O_Q = _O_Z + D_M2
_O_U = _O_Q + D_XA
_O_DT = _O_U + D_MODEL
_W_IN_COLS = _O_DT + DT_PAD

_INPROJ_STEP = 256
_VMEM_LIMIT = 56 * 1024 * 1024
_F32 = jnp.float32
_BF16 = jnp.bfloat16


def _rms_f32(xf, g):
    inv = lax.rsqrt(jnp.mean(xf * xf, axis=-1, keepdims=True) + EPS)
    return xf * inv * g


def _sigmoid(x):
    return 0.5 * jnp.tanh(0.5 * x) + 0.5


def _dot(a, b):
    return jnp.dot(a, b, preferred_element_type=_F32)


def _resident(shape):
    nd = len(shape)
    return pl.BlockSpec(shape, lambda *_: (0,) * nd, pipeline_mode=pl.Buffered(1))


def _silu(x):
    hx = 0.5 * x
    return hx * (jnp.tanh(hx) + 1.0)


def _inproj_kernel(x_ref, g_ref, w_ref, cw_ref, cb_ref, xc_ref, gates_ref, z_ref, q_ref, dt_ref, u_ref,
                   ext, halo, *, tm):
    h = _rms_f32(x_ref[...], g_ref[...]).astype(_BF16)
    step = _INPROJ_STEP
    hr = 8
    rb = 32

    def proj(lo, c, width):
        return _dot(h, w_ref[:, lo + c:lo + c + width])

    @pl.when(pl.program_id(1) == 0)
    def _():
        halo[...] = jnp.zeros_like(halo)

    def conv_epilogue(c, slot):
        taps = [cw_ref[k:k + 1, c:c + step] for k in range(M2_CONV)]
        bias = cb_ref[:, c:c + step]
        for r0 in range(0, tm, rb):
            win = ext[slot, r0:r0 + hr + rb, :]
            acc = bias + taps[M2_CONV - 1] * win[hr:]
            for k in range(1, M2_CONV):
                acc = acc + taps[M2_CONV - 1 - k] * pltpu.roll(win, k, 0)[hr:]
            xc_ref[r0:r0 + rb, c:c + step] = _silu(acc).astype(_BF16)

    def plain_epilogue(r, out_ref, c, act, dtype):
        out_ref[:, c:c + r.shape[1]] = (r if act is None else act(r)).astype(dtype)

    convs = [(_O_XBC, c, step, None) for c in range(0, D_XBC, step)]
    plains = []
    for origin, width, out_ref, act, dtype in (
            (_O_GATES, D_GATES, gates_ref, None, _BF16), (_O_Z, D_M2, z_ref, None, _BF16),
            (_O_Q, D_XA, q_ref, None, _BF16), (_O_U, D_MODEL, u_ref, None, _BF16),
            (_O_DT, DT_PAD, dt_ref, None, _F32)):
        plains += [(origin, c, min(step, width - c), (out_ref, c, act, dtype)) for c in range(0, width, step)]
    items = []
    for i in range(max(len(convs), (len(plains) + 1) // 2)):
        items += convs[i:i + 1] + plains[2 * i:2 * i + 2]

    pending = None
    for i, (origin, c, width, sink) in enumerate(items):
        r = proj(origin, c, width)
        if sink is None:
            slot = i % 2
            ext[slot, 0:hr, :] = halo[:, c:c + step]
            ext[slot, hr:hr + tm, :] = r
            halo[:, c:c + step] = r[tm - hr:tm]
            epilogue = functools.partial(conv_epilogue, c, slot)
        else:
            epilogue = functools.partial(plain_epilogue, r, *sink)
        if pending is not None:
            pending()
        pending = epilogue
    pending()


def _inproj(x2, norm_g, w_in_r, cw, cb, bsz, seq, tm):
    nt = seq // tm
    tok = bsz * seq
    row = lambda b, t: (b * nt + t, 0)
    return pl.pallas_call(
        functools.partial(_inproj_kernel, tm=tm),
        grid=(bsz, nt),
        in_specs=[
            pl.BlockSpec((tm, D_MODEL), row),
            _resident((1, D_MODEL)),
            _resident((D_MODEL, _W_IN_COLS)),
            _resident((M2_CONV, D_XBC)),
            _resident((1, D_XBC)),
        ],
        out_specs=[
            pl.BlockSpec((tm, D_XBC), row),
            pl.BlockSpec((tm, D_GATES), row),
            pl.BlockSpec((tm, D_M2), row),
            pl.BlockSpec((tm, D_XA), row),
            pl.BlockSpec((tm, DT_PAD), row),
            pl.BlockSpec((tm, D_MODEL), row),
        ],
        out_shape=[
            jax.ShapeDtypeStruct((tok, D_XBC), _BF16),
            jax.ShapeDtypeStruct((tok, D_GATES), _BF16),
            jax.ShapeDtypeStruct((tok, D_M2), _BF16),
            jax.ShapeDtypeStruct((tok, D_XA), _BF16),
            jax.ShapeDtypeStruct((tok, DT_PAD), _F32),
            jax.ShapeDtypeStruct((tok, D_MODEL), _BF16),
        ],
        scratch_shapes=[pltpu.VMEM((2, 8 + tm, _INPROJ_STEP), _F32), pltpu.VMEM((8, D_XBC), _F32)],
        compiler_params=pltpu.CompilerParams(
            dimension_semantics=("arbitrary", "arbitrary"), vmem_limit_bytes=_VMEM_LIMIT),
        name="inproj",
    )(x2, norm_g, w_in_r, cw, cb)


def _gelu_tanh(x):
    c = math.sqrt(2.0 / math.pi)
    return 0.5 * x * (1.0 + jnp.tanh(c * (x + 0.044715 * (x * x * x))))


def _s5_kernel(u_ref, unext_ref, perm_ref, permt_ref, bdb_ref, bdc_ref, are_ref, aim_ref, d_ref, out_ref,
               buf0, buf1, xb0, xb1, ub0, ub1, state, *, bsz, tt, nchunks, lane_chunk):
    ns = S5_BLOCK_STATE
    rows = bsz * tt
    half = rows // 2

    def project_in(u_bt, buf, ub):
        u2 = u_bt.reshape(rows, S5_BLOCK_CH)
        ub[0:half, :] = _dot(perm_ref[0:half, :], u2)
        ub[half:rows, :] = _dot(perm_ref[half:rows, :], u2)
        u_b = ub[...].astype(_BF16)
        buf[:, 0:ns] = _dot(u_b, bdb_ref[:, 0:ns])
        buf[:, ns:2 * ns] = _dot(u_b, bdb_ref[:, ns:2 * ns])

    def scan(buf, xb):
        for c in range(0, ns, lane_chunk):
            ar = jnp.broadcast_to(are_ref[:, c:c + lane_chunk], (bsz, lane_chunk))
            ai = jnp.broadcast_to(aim_ref[:, c:c + lane_chunk], (bsz, lane_chunk))
            xr = state[0, :, c:c + lane_chunk]
            xi = state[1, :, c:c + lane_chunk]
            for t in range(tt):
                r = slice(t * bsz, (t + 1) * bsz)
                nxr = ar * xr - ai * xi + buf[r, c:c + lane_chunk]
                nxi = ar * xi + ai * xr + buf[r, ns + c:ns + c + lane_chunk]
                xb[r, c:c + lane_chunk] = nxr.astype(_BF16)
                xb[r, ns + c:ns + c + lane_chunk] = nxi.astype(_BF16)
                xr, xi = nxr, nxi
            state[0, :, c:c + lane_chunk] = xr
            state[1, :, c:c + lane_chunk] = xi

    def project_out(xb, ub, t0):
        y = _dot(xb[:, 0:ns], bdc_ref[0:ns, :])
        y = y + _dot(xb[:, ns:2 * ns], bdc_ref[ns:2 * ns, :])
        y_tb = (y + d_ref[...] * ub[...]).astype(_BF16)
        hb = bsz // 2
        top = _dot(permt_ref[0:half, :], y_tb).astype(_BF16)
        bot = _dot(permt_ref[half:rows, :], y_tb).astype(_BF16)
        out_ref[0:hb, t0:t0 + tt, :] = top.reshape(hb, tt, S5_BLOCK_CH)
        out_ref[hb:bsz, t0:t0 + tt, :] = bot.reshape(hb, tt, S5_BLOCK_CH)

    bufs, xbs, ubs = (buf0, buf1), (xb0, xb1), (ub0, ub1)

    @pl.when(pl.program_id(1) == 0)
    def _():
        state[...] = jnp.zeros_like(state)
        project_in(u_ref[:, 0:tt, :], buf0, ub0)

    for k in range(nchunks):
        nxt = (k + 1) % 2
        if k + 1 < nchunks:
            project_in(u_ref[:, (k + 1) * tt:(k + 2) * tt, :], bufs[nxt], ubs[nxt])
        else:
            project_in(unext_ref[...], bufs[nxt], ubs[nxt])
        scan(bufs[k % 2], xbs[k % 2])
        project_out(xbs[k % 2], ubs[k % 2], k * tt)


def _s5(u, perm, permt, bdb, bdc, a_re, a_im, d_blk, bsz, seq, tt, nchunks):
    rows = tt * bsz
    nsteps = seq // (nchunks * tt)
    last = seq // tt - 1
    kern = functools.partial(_s5_kernel, bsz=bsz, tt=tt, nchunks=nchunks, lane_chunk=512)
    blk = lambda g, j: (g, 0, 0)
    return pl.pallas_call(
        kern,
        grid=(S5_BLOCKS, nsteps),
        in_specs=[
            pl.BlockSpec((bsz, nchunks * tt, S5_BLOCK_CH), lambda g, j: (0, j, g)),
            pl.BlockSpec((bsz, tt, S5_BLOCK_CH), lambda g, j: (0, jnp.minimum(nchunks * (j + 1), last), g)),
            _resident((rows, rows)),
            _resident((rows, rows)),
            pl.BlockSpec((None, S5_BLOCK_CH, 2 * S5_BLOCK_STATE), blk),
            pl.BlockSpec((None, 2 * S5_BLOCK_STATE, S5_BLOCK_CH), blk),
            pl.BlockSpec((None, 1, S5_BLOCK_STATE), blk),
            pl.BlockSpec((None, 1, S5_BLOCK_STATE), blk),
            pl.BlockSpec((None, 1, S5_BLOCK_CH), blk),
        ],
        out_specs=pl.BlockSpec((bsz, nchunks * tt, S5_BLOCK_CH), lambda g, j: (0, j, g)),
        out_shape=jax.ShapeDtypeStruct((bsz, seq, D_MODEL), _BF16),
        scratch_shapes=[
            pltpu.VMEM((rows, 2 * S5_BLOCK_STATE), _F32),
            pltpu.VMEM((rows, 2 * S5_BLOCK_STATE), _F32),
            pltpu.VMEM((rows, 2 * S5_BLOCK_STATE), _BF16),
            pltpu.VMEM((rows, 2 * S5_BLOCK_STATE), _BF16),
            pltpu.VMEM((rows, S5_BLOCK_CH), _F32),
            pltpu.VMEM((rows, S5_BLOCK_CH), _F32),
            pltpu.VMEM((2, bsz, S5_BLOCK_STATE), _F32),
        ],
        compiler_params=pltpu.CompilerParams(
            dimension_semantics=("arbitrary", "arbitrary"), vmem_limit_bytes=_VMEM_LIMIT),
        name="s5",
    )(u, u, perm, permt, bdb, bdc, a_re, a_im, d_blk)


def _softplus(x):
    return jnp.maximum(x, 0.0) + jnp.log(1.0 + jnp.exp(-jnp.abs(x)))


def _mamba_kernel(xc_ref, z_ref, dt_ref, dtb_ref, a_ref, dexp_ref, nw_ref, e2_ref, out_ref, hst, ybuf, *, nsub):
    @pl.when(pl.program_id(1) == 0)
    def _():
        hst[...] = jnp.zeros_like(hst)

    for sub in range(nsub):
        _mamba_chunk(sub * M2_CHUNK, xc_ref, z_ref, dt_ref, dtb_ref, a_ref, dexp_ref, nw_ref, e2_ref, out_ref,
                     hst, ybuf)


def _mamba_chunk(r0, xc_ref, z_ref, dt_ref, dtb_ref, a_ref, dexp_ref, nw_ref, e2_ref, out_ref, hst, ybuf):
    q = M2_CHUNK
    rs = slice(r0, r0 + q)
    bm = xc_ref[rs, D_M2:D_M2 + D_BC]
    cm = xc_ref[rs, D_M2 + D_BC:D_XBC]

    dt = _softplus(dt_ref[rs, :] + dtb_ref[...])
    da = dt * a_ref[...]
    rowi = lax.broadcasted_iota(jnp.int32, (q, q), 0)
    coli = lax.broadcasted_iota(jnp.int32, (q, q), 1)
    causal = rowi >= coli
    tril = jnp.where(causal, 1.0, 0.0).astype(_F32)
    a_cum = jnp.dot(tril, da, preferred_element_type=_F32, precision=lax.Precision.HIGHEST)
    a_last = a_cum[q - 1:q, :]
    a_cum_t = a_cum.T

    def hilo(v):
        hi = v.astype(_BF16)
        lo = (v - hi.astype(_F32)).astype(_BF16)
        return jnp.concatenate([hi, lo], axis=1)

    pad = 16
    stacked = jnp.concatenate(
        [hilo(dt), hilo(jnp.exp(a_cum)), hilo(jnp.exp(a_last - a_cum)),
         hilo(jnp.broadcast_to(jnp.exp(a_last), (pad, DT_PAD)))], axis=0)
    lane = lax.broadcasted_iota(jnp.int32, (q, 2 * M2_HEADDIM), 1)
    gw = M2_HEADS_PER_GROUP * M2_HEADDIM

    for g in range(M2_GROUPS):
        ex = _dot(stacked, e2_ref[:, g * gw:(g + 1) * gw])
        dt_e = ex[0:q]
        expa_e = ex[q:2 * q]
        dec_e = ex[2 * q:3 * q]
        cd_e = ex[3 * q:3 * q + 1]
        xdt = xc_ref[rs, g * gw:(g + 1) * gw].astype(_F32) * dt_e
        xdt_b = xdt.astype(_BF16)
        xdec_b = (xdt * dec_e).astype(_BF16)
        cg = cm[:, g * M2_STATE:(g + 1) * M2_STATE]
        bg = bm[:, g * M2_STATE:(g + 1) * M2_STATE]
        cb = lax.dot_general(cg, bg, (((1,), (1,)), ((), ())), preferred_element_type=_F32)
        hprev = hst[g]
        yoff = _dot(cg, hprev.astype(_BF16)) * expa_e
        st = lax.dot_general(bg, xdec_b, (((0,), (0,)), ((), ())), preferred_element_type=_F32)
        hst[g] = hprev * cd_e + st
        for pair in range(M2_HEADS_PER_GROUP // 2):
            h0 = g * M2_HEADS_PER_GROUP + 2 * pair
            ms = []
            for hh in (h0, h0 + 1):
                seg = a_cum[:, hh:hh + 1] - a_cum_t[hh:hh + 1, :]
                lm = jnp.exp(jnp.where(causal, seg, -1e30))
                ms.append((cb * lm).astype(_BF16))
            lhs = jnp.concatenate(ms, axis=1)
            xp = xdt_b[:, pair * 2 * M2_HEADDIM:(pair + 1) * 2 * M2_HEADDIM]
            zero = jnp.zeros_like(xp)
            rhs = jnp.concatenate([jnp.where(lane < M2_HEADDIM, xp, zero),
                                   jnp.where(lane >= M2_HEADDIM, xp, zero)], axis=0)
            yd = _dot(lhs, rhs)
            c0 = h0 * M2_HEADDIM
            ybuf[rs, c0:c0 + 2 * M2_HEADDIM] = yd + yoff[:, pair * 2 * M2_HEADDIM:(pair + 1) * 2 * M2_HEADDIM]

    y = ybuf[rs, :] + xc_ref[rs, 0:D_M2].astype(_F32) * dexp_ref[...]
    v = y * _silu(z_ref[rs, :].astype(_F32))
    out_ref[rs, :] = _rms_f32(v, nw_ref[...]).astype(_BF16)


def _mamba(xc, z, dt, dtb, a_neg, dexp, nw, e2, bsz, seq, nsub):
    q = nsub * M2_CHUNK
    nc = seq // q
    tok = bsz * seq
    row = lambda b, c: (b * nc + c, 0)
    return pl.pallas_call(
        functools.partial(_mamba_kernel, nsub=nsub),
        grid=(bsz, nc),
        in_specs=[
            pl.BlockSpec((q, D_XBC), row),
            pl.BlockSpec((q, D_M2), row),
            pl.BlockSpec((q, DT_PAD), row),
            _resident((1, DT_PAD)),
            _resident((1, DT_PAD)),
            _resident((1, D_M2)),
            _resident((1, D_M2)),
            _resident((2 * DT_PAD, D_M2)),
        ],
        out_specs=pl.BlockSpec((q, D_M2), row),
        out_shape=jax.ShapeDtypeStruct((tok, D_M2), _BF16),
        scratch_shapes=[
            pltpu.VMEM((M2_GROUPS, M2_STATE, M2_HEADS_PER_GROUP * M2_HEADDIM), _F32),
            pltpu.VMEM((q, D_M2), _F32),
        ],
        compiler_params=pltpu.CompilerParams(
            dimension_semantics=("arbitrary", "arbitrary"), vmem_limit_bytes=_VMEM_LIMIT),
        name="mamba",
    )(xc, z, dt, dtb, a_neg, dexp, nw, e2)


def _kv_kernel(mem_ref, g_ref, w_ref, kv_ref):
    mn = _rms_f32(mem_ref[...], g_ref[...]).astype(_BF16)
    kv_ref[...] = _dot(mn, w_ref[...]).astype(_BF16)


def _kv(mem2, g, w_kv, bsz):
    return pl.pallas_call(
        _kv_kernel,
        grid=(bsz,),
        in_specs=[
            pl.BlockSpec((MEM_LEN, D_MODEL), lambda b: (b, 0)),
            _resident((1, D_MODEL)),
            _resident((D_MODEL, 2 * D_XA)),
        ],
        out_specs=pl.BlockSpec((MEM_LEN, 2 * D_XA), lambda b: (b, 0)),
        out_shape=jax.ShapeDtypeStruct((bsz * MEM_LEN, 2 * D_XA), _BF16),
        compiler_params=pltpu.CompilerParams(
            dimension_semantics=("arbitrary",), vmem_limit_bytes=_VMEM_LIMIT),
        name="kv",
    )(mem2, g, w_kv)


def _attn_kernel(q_ref, kv_ref, out_ref):
    scale = XA_HEADDIM ** -0.5
    for h in range(XA_HEADS):
        lo = h * XA_HEADDIM
        qh = q_ref[:, lo:lo + XA_HEADDIM]
        kh = kv_ref[:, lo:lo + XA_HEADDIM]
        vh = kv_ref[:, D_XA + lo:D_XA + lo + XA_HEADDIM]
        s = lax.dot_general(qh, kh, (((1,), (1,)), ((), ())), preferred_element_type=_F32) * scale
        m = jnp.max(s, axis=-1, keepdims=True)
        p = jnp.exp(s - m).astype(_BF16)
        ov = _dot(p, jnp.concatenate([vh, jnp.ones_like(vh)], axis=1))
        out_ref[:, lo:lo + XA_HEADDIM] = (ov[:, 0:XA_HEADDIM] / ov[:, XA_HEADDIM:]).astype(_BF16)


def _attn(q, kv, bsz, seq, tm):
    nt = seq // tm
    row = lambda b, t: (b * nt + t, 0)
    return pl.pallas_call(
        _attn_kernel,
        grid=(bsz, nt),
        in_specs=[
            pl.BlockSpec((tm, D_XA), row),
            pl.BlockSpec((MEM_LEN, 2 * D_XA), lambda b, t: (b, 0)),
        ],
        out_specs=pl.BlockSpec((tm, D_XA), row),
        out_shape=jax.ShapeDtypeStruct((bsz * seq, D_XA), _BF16),
        compiler_params=pltpu.CompilerParams(
            dimension_semantics=("arbitrary", "arbitrary"), vmem_limit_bytes=_VMEM_LIMIT),
        name="attn",
    )(q, kv)


def _merge_kernel(x_ref, gs5_ref, gates_ref, nb_ref, at_ref, wav_ref, wag_ref, wb_ref, wc_ref, wout_ref,
                  nf_ref, x1_ref, h2_ref):
    g = _gelu_tanh(gs5_ref[...].astype(_F32)).astype(_BF16)
    ya = _dot(g, wav_ref[...]) * _sigmoid(_dot(g, wag_ref[...]))
    gate = lambda i: _sigmoid(gates_ref[:, i * D_MODEL:(i + 1) * D_MODEL].astype(_F32))
    m = gate(0) * ya
    m = m + gate(1) * _dot(nb_ref[...], wb_ref[...])
    m = m + gate(2) * _dot(at_ref[...], wc_ref[...])
    x1 = x_ref[...] + _dot(m.astype(_BF16), wout_ref[...])
    x1_ref[...] = x1
    h2_ref[...] = _rms_f32(x1, nf_ref[...]).astype(_BF16)


def _merge(x2, gs5, gates, nb, at, wav, wag, wb, wc, wout, nf, bsz, seq, tm):
    nt = seq // tm
    tok = bsz * seq
    row = lambda b, t: (b * nt + t, 0)
    return pl.pallas_call(
        _merge_kernel,
        grid=(bsz, nt),
        in_specs=[
            pl.BlockSpec((tm, D_MODEL), row),
            pl.BlockSpec((tm, D_MODEL), row),
            pl.BlockSpec((tm, D_GATES), row),
            pl.BlockSpec((tm, D_M2), row),
            pl.BlockSpec((tm, D_XA), row),
            _resident((D_MODEL, D_MODEL)),
            _resident((D_MODEL, D_MODEL)),
            _resident((D_M2, D_MODEL)),
            _resident((D_XA, D_MODEL)),
            _resident((D_MODEL, D_MODEL)),
            _resident((1, D_MODEL)),
        ],
        out_specs=[pl.BlockSpec((tm, D_MODEL), row), pl.BlockSpec((tm, D_MODEL), row)],
        out_shape=[jax.ShapeDtypeStruct((tok, D_MODEL), _F32), jax.ShapeDtypeStruct((tok, D_MODEL), _BF16)],
        compiler_params=pltpu.CompilerParams(
            dimension_semantics=("arbitrary", "arbitrary"), vmem_limit_bytes=_VMEM_LIMIT),
        name="merge",
    )(x2, gs5, gates, nb, at, wav, wag, wb, wc, wout, nf)


def _ffn_kernel(h2_ref, halo_ref, x1_ref, wup_ref, cw_ref, cb_ref, wdown_ref, nfin_ref, out_ref, act_s,
                *, tm, chunk):
    hl = 16
    first = pl.program_id(1) == 0
    halo = jnp.where(first, jnp.zeros((hl, D_MODEL), _BF16), halo_ref[...])
    h2e = jnp.concatenate([halo, h2_ref[...]], axis=0)

    def conv(u, c0):
        v = cb_ref[:, c0:c0 + chunk] + cw_ref[FFN_CONV - 1:FFN_CONV, c0:c0 + chunk] * u[hl:hl + tm]
        for k in range(1, FFN_CONV):
            tap = cw_ref[FFN_CONV - 1 - k:FFN_CONV - k, c0:c0 + chunk]
            v = v + tap * pltpu.roll(u, k, 0)[hl:hl + tm]
        return v

    def up(j):
        ca, cg = j * chunk, D_FF + j * chunk
        return _dot(h2e, wup_ref[:, ca:ca + chunk]), _dot(h2e, wup_ref[:, cg:cg + chunk])

    n = D_FF // chunk
    nxt = up(0)
    for j in range(n):
        ua, ub = nxt
        if j + 1 < n:
            nxt = up(j + 1)
        ca, cg = j * chunk, D_FF + j * chunk
        act_s[:, ca:ca + chunk] = (_silu(conv(ua, ca)) * conv(ub, cg)).astype(_BF16)
    down = _dot(act_s[...], wdown_ref[...])
    out_ref[...] = _rms_f32(x1_ref[...] + down, nfin_ref[...])


def _ffn(h2, x1, wup, cw, cb, wdown, nfin, bsz, seq, tm):
    nt = seq // tm
    tok = bsz * seq
    hl = 16
    chunk = 256
    row = lambda b, t: (b * nt + t, 0)
    halo_row = lambda b, t: (jnp.maximum((b * nt + t) * (tm // hl) - 1, 0), 0)
    kern = functools.partial(_ffn_kernel, tm=tm, chunk=chunk)
    return pl.pallas_call(
        kern,
        grid=(bsz, nt),
        in_specs=[
            pl.BlockSpec((tm, D_MODEL), row),
            pl.BlockSpec((hl, D_MODEL), halo_row),
            pl.BlockSpec((tm, D_MODEL), row),
            _resident((D_MODEL, 2 * D_FF)),
            _resident((FFN_CONV, 2 * D_FF)),
            _resident((1, 2 * D_FF)),
            _resident((D_FF, D_MODEL)),
            _resident((1, D_MODEL)),
        ],
        out_specs=pl.BlockSpec((tm, D_MODEL), row),
        out_shape=jax.ShapeDtypeStruct((tok, D_MODEL), _F32),
        scratch_shapes=[pltpu.VMEM((tm, D_FF), _BF16)],
        compiler_params=pltpu.CompilerParams(
            dimension_semantics=("arbitrary", "arbitrary"), vmem_limit_bytes=_VMEM_LIMIT),
        name="ffn",
    )(h2, h2, x1, wup, cw, cb, wdown, nfin)


def _s5_params(lam_re, lam_im, log_dt, b_re, b_im, c_re, c_im, d):
    dt = jnp.exp(log_dt)[:, None]
    mag = jnp.exp(lam_re * dt)
    ar = mag * jnp.cos(lam_im * dt)
    ai = mag * jnp.sin(lam_im * dt)
    den = lam_re * lam_re + lam_im * lam_im
    fr = ((ar - 1.0) * lam_re + ai * lam_im) / den
    fi = (ai * lam_re - (ar - 1.0) * lam_im) / den
    bbr = fr[..., None] * b_re - fi[..., None] * b_im
    bbi = fr[..., None] * b_im + fi[..., None] * b_re
    nb, gl, k, p = S5_BLOCKS, S5_GROUPS_PER_BLOCK, S5_CH_PER_GROUP, S5_STATE

    def block_diag(t, rows_per_group, cols_per_group):
        tiled = jnp.tile(t, (1, 1, gl))
        rg = jnp.arange(gl * rows_per_group)[:, None] // rows_per_group
        cg = jnp.arange(gl * cols_per_group)[None, :] // cols_per_group
        return jnp.where(rg == cg, tiled, 0.0)

    def bd_in(bb):
        return block_diag(bb.transpose(0, 2, 1).reshape(nb, gl * k, p), k, p)

    def bd_out(c):
        return block_diag(c.transpose(0, 2, 1).reshape(nb, gl * p, k), p, k)

    bdb = jnp.concatenate([bd_in(bbr), bd_in(bbi)], axis=-1).astype(_BF16)
    bdc = jnp.concatenate([bd_out(c_re), -bd_out(c_im)], axis=1).astype(_BF16)
    a_re = ar.reshape(nb, 1, gl * p)
    a_im = ai.reshape(nb, 1, gl * p)
    d_blk = d.reshape(nb, 1, gl * k)
    return bdb, bdc, a_re, a_im, d_blk


def kernel(x, mem, norm_mix, w_in, s5_lambda_re, s5_lambda_im, s5_log_dt, s5_b_re, s5_b_im, s5_c_re, s5_c_im,
           s5_d, w_a_val, w_a_gate, m2_conv_w, m2_conv_b, m2_dt_bias, m2_a_log, m2_d, m2_norm, w_b, norm_mem,
           w_kv, w_c, w_out, norm_ffn, w_up, ffn_conv_w, ffn_conv_b, w_down, norm_final):
    bsz, seq, _ = x.shape
    depth = w_in.shape[0]
    tm = min(512, seq)
    tt = 16
    s5_chunks = 16
    p1 = D_MODEL
    p2 = p1 + D_M2
    p3 = p2 + D_XBC
    p4 = p3 + M2_HEADS
    p5 = p4 + D_XA
    row1 = lambda v: v.reshape(1, -1).astype(_F32)
    head_expand = (jnp.arange(DT_PAD)[:, None] == (jnp.arange(D_M2)[None, :] // M2_HEADDIM)).astype(_BF16)
    pad_heads = lambda v: jnp.pad(v.astype(_F32), (0, DT_PAD - M2_HEADS)).reshape(1, DT_PAD)

    assert depth == 1, "single-layer problem: the final RMSNorm is fused into the FFN kernel"
    l = 0
    x2 = x.reshape(bsz * seq, D_MODEL)
    mem2 = mem.reshape(bsz * MEM_LEN, D_MODEL)
    w = w_in[l].astype(_BF16)
    w_in_r = jnp.concatenate(
        [w[:, p2:p3], w[:, p5:], w[:, p1:p2], w[:, p4:p5], w[:, :p1],
         jnp.pad(w[:, p3:p4], ((0, 0), (0, DT_PAD - M2_HEADS)))], axis=1)
    xc, gates, z, q, dt_raw, u = _inproj(x2, row1(norm_mix[l]), w_in_r, m2_conv_w[l].astype(_F32),
                                          row1(m2_conv_b[l]), bsz, seq, tm)

    bdb, bdc, a_re, a_im, d_blk = _s5_params(
        s5_lambda_re[l], s5_lambda_im[l], s5_log_dt[l], s5_b_re[l], s5_b_im[l], s5_c_re[l], s5_c_im[l],
        s5_d[l])
    ridx = jnp.arange(tt * bsz)
    perm = (((ridx % bsz) * tt + ridx // bsz)[:, None] == ridx[None, :]).astype(_BF16)
    gs5 = _s5(u.reshape(bsz, seq, D_MODEL), perm, perm.T, bdb, bdc, a_re, a_im, d_blk, bsz, seq, tt, s5_chunks)

    nb = _mamba(xc, z, dt_raw, pad_heads(m2_dt_bias[l]), pad_heads(-jnp.exp(m2_a_log[l].astype(_F32))),
                row1(jnp.repeat(m2_d[l], M2_HEADDIM)), row1(m2_norm[l]),
                jnp.concatenate([head_expand, head_expand], axis=0), bsz, seq, min(8, seq // M2_CHUNK))

    kv = _kv(mem2, row1(norm_mem[l]), w_kv[l].astype(_BF16), bsz)
    at = _attn(q, kv, bsz, seq, tm)

    x1, h2 = _merge(x2, gs5.reshape(bsz * seq, D_MODEL), gates, nb, at,
                    w_a_val[l].astype(_BF16), w_a_gate[l].astype(_BF16), w_b[l].astype(_BF16),
                    w_c[l].astype(_BF16), w_out[l].astype(_BF16), row1(norm_ffn[l]), bsz, seq, tm)

    out = _ffn(h2, x1, w_up[l].astype(_BF16), ffn_conv_w[l].astype(_F32), row1(ffn_conv_b[l]),
               w_down[l].astype(_BF16), row1(norm_final), bsz, seq, tm)
    return out.reshape(bsz, seq, D_MODEL)
```

```python
import functools
import math

import jax
import jax.numpy as jnp
from jax import lax
from jax.experimental import pallas as pl
from jax.experimental.pallas import tpu as pltpu

D_MODEL = 1024
MEM_LEN = 256
S5_CH_PER_GROUP = 16
S5_GROUPS = 64
S5_STATE = 64
S5_GROUPS_PER_BLOCK = 16
S5_BLOCKS = S5_GROUPS // S5_GROUPS_PER_BLOCK
S5_BLOCK_CH = S5_GROUPS_PER_BLOCK * S5_CH_PER_GROUP
S5_BLOCK_STATE = S5_GROUPS_PER_BLOCK * S5_STATE
D_M2 = 2048
M2_HEADDIM = 64
M2_HEADS = 32
M2_GROUPS = 4
M2_HEADS_PER_GROUP = M2_HEADS // M2_GROUPS
M2_STATE = 128
M2_CONV = 4
M2_CHUNK = 128
D_BC = M2_GROUPS * M2_STATE
D_XBC = D_M2 + 2 * D_BC
XA_HEADS = 4
XA_HEADDIM = 128
D_XA = 512
D_FF = 2816
FFN_CONV = 3
D_GATES = 3 * D_MODEL
DT_PAD = 128
EPS = 1e-6

_O_U = 0
_O_Z = _O_U + D_MODEL
_O_XBC = _O_Z + D_M2
_O_Q = _O_XBC + D_XBC
_O_GATES = _O_Q + D_XA
_O_DT = _O_GATES + D_GATES
_W_IN_COLS = _O_DT + DT_PAD

_INPROJ_STEP = 256
_VMEM_LIMIT = 56 * 1024 * 1024
_F32 = jnp.float32
_BF16 = jnp.bfloat16


def _rms_f32(xf, g):
    inv = lax.rsqrt(jnp.mean(xf * xf, axis=-1, keepdims=True) + EPS)
    return xf * inv * g


def _sigmoid(x):
    return 0.5 * jnp.tanh(0.5 * x) + 0.5


def _dot(a, b):
    return jnp.dot(a, b, preferred_element_type=_F32)


def _resident(shape):
    nd = len(shape)
    return pl.BlockSpec(shape, lambda *_: (0,) * nd, pipeline_mode=pl.Buffered(1))


def _silu(x):
    hx = 0.5 * x
    return hx * (jnp.tanh(hx) + 1.0)


def _inproj_kernel(x_ref, g_ref, w_ref, cw_ref, cb_ref, xc_ref, gates_ref, z_ref, q_ref, dt_ref, u_ref,
                   ext, halo, *, tm):
    h = _rms_f32(x_ref[...], g_ref[...]).astype(_BF16)
    step = _INPROJ_STEP
    hr = 8
    rb = 32

    def proj(lo, c, width):
        return _dot(h, w_ref[:, lo + c:lo + c + width])

    @pl.when(pl.program_id(1) == 0)
    def _():
        halo[...] = jnp.zeros_like(halo)

    def conv_epilogue(c, slot):
        taps = [cw_ref[k:k + 1, c:c + step] for k in range(M2_CONV)]
        bias = cb_ref[:, c:c + step]
        for r0 in range(0, tm, rb):
            win = ext[slot, r0:r0 + hr + rb, :]
            acc = bias + taps[M2_CONV - 1] * win[hr:]
            for k in range(1, M2_CONV):
                acc = acc + taps[M2_CONV - 1 - k] * pltpu.roll(win, k, 0)[hr:]
            xc_ref[r0:r0 + rb, c:c + step] = _silu(acc).astype(_BF16)

    def plain_epilogue(r, out_ref, c, act, dtype):
        out_ref[:, c:c + r.shape[1]] = (r if act is None else act(r)).astype(dtype)

    convs = [(_O_XBC, c, step, None) for c in range(0, D_XBC, step)]
    plains = []
    for origin, width, out_ref, act, dtype in (
            (_O_GATES, D_GATES, gates_ref, None, _BF16), (_O_Z, D_M2, z_ref, None, _BF16),
            (_O_Q, D_XA, q_ref, None, _BF16), (_O_U, D_MODEL, u_ref, None, _BF16),
            (_O_DT, DT_PAD, dt_ref, None, _F32)):
        plains += [(origin, c, min(step, width - c), (out_ref, c, act, dtype)) for c in range(0, width, step)]
    items = []
    for i in range(max(len(convs), (len(plains) + 1) // 2)):
        items += convs[i:i + 1] + plains[2 * i:2 * i + 2]

    pending = None
    for i, (origin, c, width, sink) in enumerate(items):
        r = proj(origin, c, width)
        if sink is None:
            slot = i % 2
            ext[slot, 0:hr, :] = halo[:, c:c + step]
            ext[slot, hr:hr + tm, :] = r
            halo[:, c:c + step] = r[tm - hr:tm]
            epilogue = functools.partial(conv_epilogue, c, slot)
        else:
            epilogue = functools.partial(plain_epilogue, r, *sink)
        if pending is not None:
            pending()
        pending = epilogue
    pending()


def _inproj(x2, norm_g, w_in_r, cw, cb, bsz, seq, tm):
    nt = seq // tm
    tok = bsz * seq
    row = lambda b, t: (b * nt + t, 0)
    return pl.pallas_call(
        functools.partial(_inproj_kernel, tm=tm),
        grid=(bsz, nt),
        in_specs=[
            pl.BlockSpec((tm, D_MODEL), row),
            _resident((1, D_MODEL)),
            _resident((D_MODEL, _W_IN_COLS)),
            _resident((M2_CONV, D_XBC)),
            _resident((1, D_XBC)),
        ],
        out_specs=[
            pl.BlockSpec((tm, D_XBC), row),
            pl.BlockSpec((tm, D_GATES), row),
            pl.BlockSpec((tm, D_M2), row),
            pl.BlockSpec((tm, D_XA), row),
            pl.BlockSpec((tm, DT_PAD), row),
            pl.BlockSpec((tm, D_MODEL), row),
        ],
        out_shape=[
            jax.ShapeDtypeStruct((tok, D_XBC), _BF16),
            jax.ShapeDtypeStruct((tok, D_GATES), _BF16),
            jax.ShapeDtypeStruct((tok, D_M2), _BF16),
            jax.ShapeDtypeStruct((tok, D_XA), _BF16),
            jax.ShapeDtypeStruct((tok, DT_PAD), _F32),
            jax.ShapeDtypeStruct((tok, D_MODEL), _BF16),
        ],
        scratch_shapes=[pltpu.VMEM((2, 8 + tm, _INPROJ_STEP), _F32), pltpu.VMEM((8, D_XBC), _F32)],
        compiler_params=pltpu.CompilerParams(
            dimension_semantics=("arbitrary", "arbitrary"), vmem_limit_bytes=_VMEM_LIMIT),
        name="inproj",
    )(x2, norm_g, w_in_r, cw, cb)


def _gelu_tanh(x):
    c = math.sqrt(2.0 / math.pi)
    return 0.5 * x * (1.0 + jnp.tanh(c * (x + 0.044715 * (x * x * x))))


def _s5_kernel(u_ref, unext_ref, perm_ref, permt_ref, bdb_ref, bdc_ref, are_ref, aim_ref, d_ref, out_ref,
               buf0, buf1, xb0, xb1, ub0, ub1, state, *, bsz, tt, nchunks, lane_chunk):
    ns = S5_BLOCK_STATE
    rows = bsz * tt
    half = rows // 2

    def project_in(u_bt, buf, ub):
        u2 = u_bt.reshape(rows, S5_BLOCK_CH)
        ub[0:half, :] = _dot(perm_ref[0:half, :], u2)
        ub[half:rows, :] = _dot(perm_ref[half:rows, :], u2)
        u_b = ub[...].astype(_BF16)
        buf[:, 0:ns] = _dot(u_b, bdb_ref[:, 0:ns])
        buf[:, ns:2 * ns] = _dot(u_b, bdb_ref[:, ns:2 * ns])

    def scan(buf, xb):
        for c in range(0, ns, lane_chunk):
            ar = jnp.broadcast_to(are_ref[:, c:c + lane_chunk], (bsz, lane_chunk))
            ai = jnp.broadcast_to(aim_ref[:, c:c + lane_chunk], (bsz, lane_chunk))
            xr = state[0, :, c:c + lane_chunk]
            xi = state[1, :, c:c + lane_chunk]
            for t in range(tt):
                r = slice(t * bsz, (t + 1) * bsz)
                nxr = ar * xr - ai * xi + buf[r, c:c + lane_chunk]
                nxi = ar * xi + ai * xr + buf[r, ns + c:ns + c + lane_chunk]
                xb[r, c:c + lane_chunk] = nxr.astype(_BF16)
                xb[r, ns + c:ns + c + lane_chunk] = nxi.astype(_BF16)
                xr, xi = nxr, nxi
            state[0, :, c:c + lane_chunk] = xr
            state[1, :, c:c + lane_chunk] = xi

    def project_out(xb, ub, t0):
        y = _dot(xb[:, 0:ns], bdc_ref[0:ns, :])
        y = y + _dot(xb[:, ns:2 * ns], bdc_ref[ns:2 * ns, :])
        y_tb = (y + d_ref[...] * ub[...]).astype(_BF16)
        hb = bsz // 2
        top = _dot(permt_ref[0:half, :], y_tb).astype(_BF16)
        bot = _dot(permt_ref[half:rows, :], y_tb).astype(_BF16)
        out_ref[0:hb, t0:t0 + tt, :] = top.reshape(hb, tt, S5_BLOCK_CH)
        out_ref[hb:bsz, t0:t0 + tt, :] = bot.reshape(hb, tt, S5_BLOCK_CH)

    bufs, xbs, ubs = (buf0, buf1), (xb0, xb1), (ub0, ub1)

    @pl.when(pl.program_id(1) == 0)
    def _():
        state[...] = jnp.zeros_like(state)
        project_in(u_ref[:, 0:tt, :], buf0, ub0)

    for k in range(nchunks):
        nxt = (k + 1) % 2
        if k + 1 < nchunks:
            project_in(u_ref[:, (k + 1) * tt:(k + 2) * tt, :], bufs[nxt], ubs[nxt])
        else:
            project_in(unext_ref[...], bufs[nxt], ubs[nxt])
        scan(bufs[k % 2], xbs[k % 2])
        project_out(xbs[k % 2], ubs[k % 2], k * tt)


def _s5(u, perm, permt, bdb, bdc, a_re, a_im, d_blk, bsz, seq, tt, nchunks):
    rows = tt * bsz
    nsteps = seq // (nchunks * tt)
    last = seq // tt - 1
    kern = functools.partial(_s5_kernel, bsz=bsz, tt=tt, nchunks=nchunks, lane_chunk=512)
    blk = lambda g, j: (g, 0, 0)
    return pl.pallas_call(
        kern,
        grid=(S5_BLOCKS, nsteps),
        in_specs=[
            pl.BlockSpec((bsz, nchunks * tt, S5_BLOCK_CH), lambda g, j: (0, j, g)),
            pl.BlockSpec((bsz, tt, S5_BLOCK_CH), lambda g, j: (0, jnp.minimum(nchunks * (j + 1), last), g)),
            _resident((rows, rows)),
            _resident((rows, rows)),
            pl.BlockSpec((None, S5_BLOCK_CH, 2 * S5_BLOCK_STATE), blk),
            pl.BlockSpec((None, 2 * S5_BLOCK_STATE, S5_BLOCK_CH), blk),
            pl.BlockSpec((None, 1, S5_BLOCK_STATE), blk),
            pl.BlockSpec((None, 1, S5_BLOCK_STATE), blk),
            pl.BlockSpec((None, 1, S5_BLOCK_CH), blk),
        ],
        out_specs=pl.BlockSpec((bsz, nchunks * tt, S5_BLOCK_CH), lambda g, j: (0, j, g)),
        out_shape=jax.ShapeDtypeStruct((bsz, seq, D_MODEL), _BF16),
        scratch_shapes=[
            pltpu.VMEM((rows, 2 * S5_BLOCK_STATE), _F32),
            pltpu.VMEM((rows, 2 * S5_BLOCK_STATE), _F32),
            pltpu.VMEM((rows, 2 * S5_BLOCK_STATE), _BF16),
            pltpu.VMEM((rows, 2 * S5_BLOCK_STATE), _BF16),
            pltpu.VMEM((rows, S5_BLOCK_CH), _F32),
            pltpu.VMEM((rows, S5_BLOCK_CH), _F32),
            pltpu.VMEM((2, bsz, S5_BLOCK_STATE), _F32),
        ],
        compiler_params=pltpu.CompilerParams(
            dimension_semantics=("arbitrary", "arbitrary"), vmem_limit_bytes=_VMEM_LIMIT),
        name="s5",
    )(u, u, perm, permt, bdb, bdc, a_re, a_im, d_blk)


def _softplus(x):
    return jnp.maximum(x, 0.0) + jnp.log(1.0 + jnp.exp(-jnp.abs(x)))


def _mamba_kernel(xc_ref, z_ref, dt_ref, dtb_ref, a_ref, dexp_ref, nw_ref, e2_ref, out_ref, hst, ybuf, *, nsub):
    @pl.when(pl.program_id(1) == 0)
    def _():
        hst[...] = jnp.zeros_like(hst)

    for sub in range(nsub):
        _mamba_chunk(sub * M2_CHUNK, xc_ref, z_ref, dt_ref, dtb_ref, a_ref, dexp_ref, nw_ref, e2_ref, out_ref,
                     hst, ybuf)


def _mamba_chunk(r0, xc_ref, z_ref, dt_ref, dtb_ref, a_ref, dexp_ref, nw_ref, e2_ref, out_ref, hst, ybuf):
    q = M2_CHUNK
    rs = slice(r0, r0 + q)
    bm = xc_ref[rs, D_M2:D_M2 + D_BC]
    cm = xc_ref[rs, D_M2 + D_BC:D_XBC]

    dt = _softplus(dt_ref[rs, :] + dtb_ref[...])
    da = dt * a_ref[...]
    rowi = lax.broadcasted_iota(jnp.int32, (q, q), 0)
    coli = lax.broadcasted_iota(jnp.int32, (q, q), 1)
    causal = rowi >= coli
    tril = jnp.where(causal, 1.0, 0.0).astype(_F32)
    a_cum = jnp.dot(tril, da, preferred_element_type=_F32, precision=lax.Precision.HIGHEST)
    a_last = a_cum[q - 1:q, :]
    a_cum_t = a_cum.T

    def hilo(v):
        hi = v.astype(_BF16)
        lo = (v - hi.astype(_F32)).astype(_BF16)
        return jnp.concatenate([hi, lo], axis=1)

    pad = 16
    stacked = jnp.concatenate(
        [hilo(dt), hilo(jnp.exp(a_cum)), hilo(jnp.exp(a_last - a_cum)),
         hilo(jnp.broadcast_to(jnp.exp(a_last), (pad, DT_PAD)))], axis=0)
    lane = lax.broadcasted_iota(jnp.int32, (q, 2 * M2_HEADDIM), 1)
    gw = M2_HEADS_PER_GROUP * M2_HEADDIM

    for g in range(M2_GROUPS):
        ex = _dot(stacked, e2_ref[:, g * gw:(g + 1) * gw])
        dt_e = ex[0:q]
        expa_e = ex[q:2 * q]
        dec_e = ex[2 * q:3 * q]
        cd_e = ex[3 * q:3 * q + 1]
        xdt = xc_ref[rs, g * gw:(g + 1) * gw].astype(_F32) * dt_e
        xdt_b = xdt.astype(_BF16)
        xdec_b = (xdt * dec_e).astype(_BF16)
        cg = cm[:, g * M2_STATE:(g + 1) * M2_STATE]
        bg = bm[:, g * M2_STATE:(g + 1) * M2_STATE]
        cb = lax.dot_general(cg, bg, (((1,), (1,)), ((), ())), preferred_element_type=_F32)
        hprev = hst[g]
        yoff = _dot(cg, hprev.astype(_BF16)) * expa_e
        st = lax.dot_general(bg, xdec_b, (((0,), (0,)), ((), ())), preferred_element_type=_F32)
        hst[g] = hprev * cd_e + st
        for pair in range(M2_HEADS_PER_GROUP // 2):
            h0 = g * M2_HEADS_PER_GROUP + 2 * pair
            ms = []
            for hh in (h0, h0 + 1):
                seg = a_cum[:, hh:hh + 1] - a_cum_t[hh:hh + 1, :]
                lm = jnp.exp(jnp.where(causal, seg, -1e30))
                ms.append((cb * lm).astype(_BF16))
            lhs = jnp.concatenate(ms, axis=1)
            xp = xdt_b[:, pair * 2 * M2_HEADDIM:(pair + 1) * 2 * M2_HEADDIM]
            zero = jnp.zeros_like(xp)
            rhs = jnp.concatenate([jnp.where(lane < M2_HEADDIM, xp, zero),
                                   jnp.where(lane >= M2_HEADDIM, xp, zero)], axis=0)
            yd = _dot(lhs, rhs)
            c0 = h0 * M2_HEADDIM
            ybuf[rs, c0:c0 + 2 * M2_HEADDIM] = yd + yoff[:, pair * 2 * M2_HEADDIM:(pair + 1) * 2 * M2_HEADDIM]

    y = ybuf[rs, :] + xc_ref[rs, 0:D_M2].astype(_F32) * dexp_ref[...]
    v = y * _silu(z_ref[rs, :].astype(_F32))
    out_ref[rs, :] = _rms_f32(v, nw_ref[...]).astype(_BF16)


def _mamba(xc, z, dt, dtb, a_neg, dexp, nw, e2, bsz, seq, nsub):
    q = nsub * M2_CHUNK
    nc = seq // q
    tok = bsz * seq
    row = lambda b, c: (b * nc + c, 0)
    return pl.pallas_call(
        functools.partial(_mamba_kernel, nsub=nsub),
        grid=(bsz, nc),
        in_specs=[
            pl.BlockSpec((q, D_XBC), row),
            pl.BlockSpec((q, D_M2), row),
            pl.BlockSpec((q, DT_PAD), row),
            _resident((1, DT_PAD)),
            _resident((1, DT_PAD)),
            _resident((1, D_M2)),
            _resident((1, D_M2)),
            _resident((2 * DT_PAD, D_M2)),
        ],
        out_specs=pl.BlockSpec((q, D_M2), row),
        out_shape=jax.ShapeDtypeStruct((tok, D_M2), _BF16),
        scratch_shapes=[
            pltpu.VMEM((M2_GROUPS, M2_STATE, M2_HEADS_PER_GROUP * M2_HEADDIM), _F32),
            pltpu.VMEM((q, D_M2), _F32),
        ],
        compiler_params=pltpu.CompilerParams(
            dimension_semantics=("arbitrary", "arbitrary"), vmem_limit_bytes=_VMEM_LIMIT),
        name="mamba",
    )(xc, z, dt, dtb, a_neg, dexp, nw, e2)


def _kv_kernel(mem_ref, g_ref, w_ref, kv_ref):
    mn = _rms_f32(mem_ref[...], g_ref[...]).astype(_BF16)
    kv_ref[...] = _dot(mn, w_ref[...]).astype(_BF16)


def _kv(mem2, g, w_kv, bsz):
    return pl.pallas_call(
        _kv_kernel,
        grid=(bsz,),
        in_specs=[
            pl.BlockSpec((MEM_LEN, D_MODEL), lambda b: (b, 0)),
            _resident((1, D_MODEL)),
            _resident((D_MODEL, 2 * D_XA)),
        ],
        out_specs=pl.BlockSpec((MEM_LEN, 2 * D_XA), lambda b: (b, 0)),
        out_shape=jax.ShapeDtypeStruct((bsz * MEM_LEN, 2 * D_XA), _BF16),
        compiler_params=pltpu.CompilerParams(
            dimension_semantics=("arbitrary",), vmem_limit_bytes=_VMEM_LIMIT),
        name="kv",
    )(mem2, g, w_kv)


def _attn_kernel(q_ref, kv_ref, out_ref):
    scale = XA_HEADDIM ** -0.5
    for h in range(XA_HEADS):
        lo = h * XA_HEADDIM
        qh = q_ref[:, lo:lo + XA_HEADDIM]
        kh = kv_ref[:, lo:lo + XA_HEADDIM]
        vh = kv_ref[:, D_XA + lo:D_XA + lo + XA_HEADDIM]
        s = lax.dot_general(qh, kh, (((1,), (1,)), ((), ())), preferred_element_type=_F32) * scale
        m = jnp.max(s, axis=-1, keepdims=True)
        p = jnp.exp(s - m).astype(_BF16)
        ov = _dot(p, jnp.concatenate([vh, jnp.ones_like(vh)], axis=1))
        out_ref[:, lo:lo + XA_HEADDIM] = (ov[:, 0:XA_HEADDIM] / ov[:, XA_HEADDIM:]).astype(_BF16)


def _attn(q, kv, bsz, seq, tm):
    nt = seq // tm
    row = lambda b, t: (b * nt + t, 0)
    return pl.pallas_call(
        _attn_kernel,
        grid=(bsz, nt),
        in_specs=[
            pl.BlockSpec((tm, D_XA), row),
            pl.BlockSpec((MEM_LEN, 2 * D_XA), lambda b, t: (b, 0)),
        ],
        out_specs=pl.BlockSpec((tm, D_XA), row),
        out_shape=jax.ShapeDtypeStruct((bsz * seq, D_XA), _BF16),
        compiler_params=pltpu.CompilerParams(
            dimension_semantics=("arbitrary", "arbitrary"), vmem_limit_bytes=_VMEM_LIMIT),
        name="attn",
    )(q, kv)


def _merge_kernel(x_ref, gs5_ref, gates_ref, nb_ref, at_ref, wav_ref, wag_ref, wb_ref, wc_ref, wout_ref,
                  nf_ref, x1_ref, h2_ref):
    g = _gelu_tanh(gs5_ref[...].astype(_F32)).astype(_BF16)
    ya = _dot(g, wav_ref[...]) * _sigmoid(_dot(g, wag_ref[...]))
    gate = lambda i: _sigmoid(gates_ref[:, i * D_MODEL:(i + 1) * D_MODEL].astype(_F32))
    m = gate(0) * ya
    m = m + gate(1) * _dot(nb_ref[...], wb_ref[...])
    m = m + gate(2) * _dot(at_ref[...], wc_ref[...])
    x1 = x_ref[...] + _dot(m.astype(_BF16), wout_ref[...])
    x1_ref[...] = x1
    h2_ref[...] = _rms_f32(x1, nf_ref[...]).astype(_BF16)


def _merge(x2, gs5, gates, nb, at, wav, wag, wb, wc, wout, nf, bsz, seq, tm):
    nt = seq // tm
    tok = bsz * seq
    row = lambda b, t: (b * nt + t, 0)
    return pl.pallas_call(
        _merge_kernel,
        grid=(bsz, nt),
        in_specs=[
            pl.BlockSpec((tm, D_MODEL), row),
            pl.BlockSpec((tm, D_MODEL), row),
            pl.BlockSpec((tm, D_GATES), row),
            pl.BlockSpec((tm, D_M2), row),
            pl.BlockSpec((tm, D_XA), row),
            _resident((D_MODEL, D_MODEL)),
            _resident((D_MODEL, D_MODEL)),
            _resident((D_M2, D_MODEL)),
            _resident((D_XA, D_MODEL)),
            _resident((D_MODEL, D_MODEL)),
            _resident((1, D_MODEL)),
        ],
        out_specs=[pl.BlockSpec((tm, D_MODEL), row), pl.BlockSpec((tm, D_MODEL), row)],
        out_shape=[jax.ShapeDtypeStruct((tok, D_MODEL), _F32), jax.ShapeDtypeStruct((tok, D_MODEL), _BF16)],
        compiler_params=pltpu.CompilerParams(
            dimension_semantics=("arbitrary", "arbitrary"), vmem_limit_bytes=_VMEM_LIMIT),
        name="merge",
    )(x2, gs5, gates, nb, at, wav, wag, wb, wc, wout, nf)


def _ffn_kernel(h2_ref, halo_ref, x1_ref, wup_ref, cw_ref, cb_ref, wdown_ref, nfin_ref, out_ref, act_s,
                *, tm, chunk):
    hl = 16
    first = pl.program_id(1) == 0
    halo = jnp.where(first, jnp.zeros((hl, D_MODEL), _BF16), halo_ref[...])
    h2e = jnp.concatenate([halo, h2_ref[...]], axis=0)

    def conv(u, c0):
        v = cb_ref[:, c0:c0 + chunk] + cw_ref[FFN_CONV - 1:FFN_CONV, c0:c0 + chunk] * u[hl:hl + tm]
        for k in range(1, FFN_CONV):
            tap = cw_ref[FFN_CONV - 1 - k:FFN_CONV - k, c0:c0 + chunk]
            v = v + tap * pltpu.roll(u, k, 0)[hl:hl + tm]
        return v

    def up(j):
        ca, cg = j * chunk, D_FF + j * chunk
        return _dot(h2e, wup_ref[:, ca:ca + chunk]), _dot(h2e, wup_ref[:, cg:cg + chunk])

    n = D_FF // chunk
    nxt = up(0)
    for j in range(n):
        ua, ub = nxt
        if j + 1 < n:
            nxt = up(j + 1)
        ca, cg = j * chunk, D_FF + j * chunk
        act_s[:, ca:ca + chunk] = (_silu(conv(ua, ca)) * conv(ub, cg)).astype(_BF16)
    down = _dot(act_s[...], wdown_ref[...])
    out_ref[...] = _rms_f32(x1_ref[...] + down, nfin_ref[...])


def _ffn(h2, x1, wup, cw, cb, wdown, nfin, bsz, seq, tm):
    nt = seq // tm
    tok = bsz * seq
    hl = 16
    chunk = 256
    row = lambda b, t: (b * nt + t, 0)
    halo_row = lambda b, t: (jnp.maximum((b * nt + t) * (tm // hl) - 1, 0), 0)
    kern = functools.partial(_ffn_kernel, tm=tm, chunk=chunk)
    return pl.pallas_call(
        kern,
        grid=(bsz, nt),
        in_specs=[
            pl.BlockSpec((tm, D_MODEL), row),
            pl.BlockSpec((hl, D_MODEL), halo_row),
            pl.BlockSpec((tm, D_MODEL), row),
            _resident((D_MODEL, 2 * D_FF)),
            _resident((FFN_CONV, 2 * D_FF)),
            _resident((1, 2 * D_FF)),
            _resident((D_FF, D_MODEL)),
            _resident((1, D_MODEL)),
        ],
        out_specs=pl.BlockSpec((tm, D_MODEL), row),
        out_shape=jax.ShapeDtypeStruct((tok, D_MODEL), _F32),
        scratch_shapes=[pltpu.VMEM((tm, D_FF), _BF16)],
        compiler_params=pltpu.CompilerParams(
            dimension_semantics=("arbitrary", "arbitrary"), vmem_limit_bytes=_VMEM_LIMIT),
        name="ffn",
    )(h2, h2, x1, wup, cw, cb, wdown, nfin)


def _s5_params(lam_re, lam_im, log_dt, b_re, b_im, c_re, c_im, d):
    dt = jnp.exp(log_dt)[:, None]
    mag = jnp.exp(lam_re * dt)
    ar = mag * jnp.cos(lam_im * dt)
    ai = mag * jnp.sin(lam_im * dt)
    den = lam_re * lam_re + lam_im * lam_im
    fr = ((ar - 1.0) * lam_re + ai * lam_im) / den
    fi = (ai * lam_re - (ar - 1.0) * lam_im) / den
    bbr = fr[..., None] * b_re - fi[..., None] * b_im
    bbi = fr[..., None] * b_im + fi[..., None] * b_re
    nb, gl, k, p = S5_BLOCKS, S5_GROUPS_PER_BLOCK, S5_CH_PER_GROUP, S5_STATE

    def block_diag(t, rows_per_group, cols_per_group):
        tiled = jnp.tile(t, (1, 1, gl))
        rg = jnp.arange(gl * rows_per_group)[:, None] // rows_per_group
        cg = jnp.arange(gl * cols_per_group)[None, :] // cols_per_group
        return jnp.where(rg == cg, tiled, 0.0)

    def bd_in(bb):
        return block_diag(bb.transpose(0, 2, 1).reshape(nb, gl * k, p), k, p)

    def bd_out(c):
        return block_diag(c.transpose(0, 2, 1).reshape(nb, gl * p, k), p, k)

    bdb = jnp.concatenate([bd_in(bbr), bd_in(bbi)], axis=-1).astype(_BF16)
    bdc = jnp.concatenate([bd_out(c_re), -bd_out(c_im)], axis=1).astype(_BF16)
    a_re = ar.reshape(nb, 1, gl * p)
    a_im = ai.reshape(nb, 1, gl * p)
    d_blk = d.reshape(nb, 1, gl * k)
    return bdb, bdc, a_re, a_im, d_blk


def kernel(x, mem, norm_mix, w_in, s5_lambda_re, s5_lambda_im, s5_log_dt, s5_b_re, s5_b_im, s5_c_re, s5_c_im,
           s5_d, w_a_val, w_a_gate, m2_conv_w, m2_conv_b, m2_dt_bias, m2_a_log, m2_d, m2_norm, w_b, norm_mem,
           w_kv, w_c, w_out, norm_ffn, w_up, ffn_conv_w, ffn_conv_b, w_down, norm_final):
    bsz, seq, _ = x.shape
    depth = w_in.shape[0]
    tm = min(512, seq)
    tt = 16
    s5_chunks = 16
    p1 = D_MODEL
    p2 = p1 + D_M2
    p3 = p2 + D_XBC
    p4 = p3 + M2_HEADS
    p5 = p4 + D_XA
    row1 = lambda v: v.reshape(1, -1).astype(_F32)
    head_expand = (jnp.arange(DT_PAD)[:, None] == (jnp.arange(D_M2)[None, :] // M2_HEADDIM)).astype(_BF16)
    pad_heads = lambda v: jnp.pad(v.astype(_F32), (0, DT_PAD - M2_HEADS)).reshape(1, DT_PAD)

    assert depth == 1, "single-layer problem: the final RMSNorm is fused into the FFN kernel"
    l = 0
    x2 = x.reshape(bsz * seq, D_MODEL)
    mem2 = mem.reshape(bsz * MEM_LEN, D_MODEL)
    w = w_in[l].astype(_BF16)
    w_in_r = jnp.concatenate(
        [w[:, :p3], w[:, p4:], jnp.pad(w[:, p3:p4], ((0, 0), (0, DT_PAD - M2_HEADS)))], axis=1)
    xc, gates, z, q, dt_raw, u = _inproj(x2, row1(norm_mix[l]), w_in_r, m2_conv_w[l].astype(_F32),
                                          row1(m2_conv_b[l]), bsz, seq, tm)

    bdb, bdc, a_re, a_im, d_blk = _s5_params(
        s5_lambda_re[l], s5_lambda_im[l], s5_log_dt[l], s5_b_re[l], s5_b_im[l], s5_c_re[l], s5_c_im[l],
        s5_d[l])
    ridx = jnp.arange(tt * bsz)
    perm = (((ridx % bsz) * tt + ridx // bsz)[:, None] == ridx[None, :]).astype(_BF16)
    gs5 = _s5(u.reshape(bsz, seq, D_MODEL), perm, perm.T, bdb, bdc, a_re, a_im, d_blk, bsz, seq, tt, s5_chunks)

    nb = _mamba(xc, z, dt_raw, pad_heads(m2_dt_bias[l]), pad_heads(-jnp.exp(m2_a_log[l].astype(_F32))),
                row1(jnp.repeat(m2_d[l], M2_HEADDIM)), row1(m2_norm[l]),
                jnp.concatenate([head_expand, head_expand], axis=0), bsz, seq, min(8, seq // M2_CHUNK))

    kv = _kv(mem2, row1(norm_mem[l]), w_kv[l].astype(_BF16), bsz)
    at = _attn(q, kv, bsz, seq, tm)

    x1, h2 = _merge(x2, gs5.reshape(bsz * seq, D_MODEL), gates, nb, at,
                    w_a_val[l].astype(_BF16), w_a_gate[l].astype(_BF16), w_b[l].astype(_BF16),
                    w_c[l].astype(_BF16), w_out[l].astype(_BF16), row1(norm_ffn[l]), bsz, seq, tm)

    out = _ffn(h2, x1, w_up[l].astype(_BF16), ffn_conv_w[l].astype(_F32), row1(ffn_conv_b[l]),
               w_down[l].astype(_BF16), row1(norm_final), bsz, seq, tm)
    return out.reshape(bsz, seq, D_MODEL)
```
